```python
import math
import jax, jax.numpy as jnp
from jax import lax
import numpy as np

D_MODEL = 1024
BATCH = 4
SEQ = 4096
DEPTH = 4
DEC_BATCH = 16
DEC_SEQ = 32
PAST_LEN = 1024

CHUNK = 64
N_MIXERS = 2
N_CONV = (DEPTH + 1) // 2
N_ATTN = DEPTH // 2
N_HEADS = 8
HEAD_DIM = 64
V_DIM = 2 * HEAD_DIM
D_FF = 2816
CONV_WIDTH = 31
CONV_STATE = CONV_WIDTH - 1
Q_BLOCK = 128
EPS = 1e-6
FFN_RES = 0.5

kernel_name = "streaming_conformer_diffattn_hybrid"


def lambda_init(layer):
    return 0.8 - 0.6 * math.exp(-0.3 * layer)


def rms_norm(x, g):
    x32 = x.astype(jnp.float32)
    y = x32 * lax.rsqrt(jnp.mean(x32 * x32, axis=-1, keepdims=True) + EPS)
    return (y * g.astype(jnp.float32)).astype(x.dtype)


def layer_norm(x, g, b):
    x32 = x.astype(jnp.float32)
    mu = jnp.mean(x32, axis=-1, keepdims=True)
    xc = x32 - mu
    y = xc * lax.rsqrt(jnp.mean(xc * xc, axis=-1, keepdims=True) + EPS)
    return (y * g.astype(jnp.float32) + b.astype(jnp.float32)).astype(x.dtype)


def swiglu(h, w_gate, w_up, w_down):
    return (jax.nn.silu(h @ w_gate) * (h @ w_up)) @ w_down


def conv_module(h, past, w_pw1, b_pw1, w_dw, b_dw, ln_g, ln_b, w_pw2, b_pw2):
    a = h @ w_pw1 + b_pw1
    u = a[..., :D_MODEL] * jax.nn.sigmoid(a[..., D_MODEL:])
    u_ext = jnp.concatenate([past.astype(u.dtype), u], axis=1)
    c = lax.conv_general_dilated(
        u_ext, w_dw[:, None, :].astype(u.dtype), window_strides=(1,), padding='VALID',
        dimension_numbers=('NWC', 'WIO', 'NWC'), feature_group_count=D_MODEL) + b_dw
    c = jax.nn.silu(layer_norm(c, ln_g, ln_b))
    return c @ w_pw2 + b_pw2, u_ext[:, -CONV_STATE:]


def diff_qkv(h, w_qkv):
    B, L, _ = h.shape
    q, k, v = jnp.split(h @ w_qkv, 3, axis=-1)
    q = q.reshape(B, L, 2 * N_HEADS, HEAD_DIM)
    k = k.reshape(B, L, 2 * N_HEADS, HEAD_DIM)
    v = v.reshape(B, L, N_HEADS, V_DIM)
    return q, k, v


def diff_attend(q, k, v, q_pos, k_pos, lam):
    s = jnp.einsum('bqhcd,bkhcd->bhcqk', q, k).astype(jnp.float32) * (HEAD_DIM ** -0.5)
    mask = (k_pos[None, :] // CHUNK) <= (q_pos[:, None] // CHUNK)
    p = jax.nn.softmax(jnp.where(mask, s, -jnp.inf), axis=-1)
    a = p[:, :, 0] - lam * p[:, :, 1]
    return jnp.einsum('bhqk,bkhe->bqhe', a.astype(v.dtype), v)


def diff_attend_blocked(q, k, v, lam):
    B, S = q.shape[:2]
    nb = S // Q_BLOCK
    qb = jnp.moveaxis(q.reshape(B, nb, Q_BLOCK, N_HEADS, 2, HEAD_DIM), 1, 0)
    pos = jnp.arange(S, dtype=jnp.int32)
    posb = pos.reshape(nb, Q_BLOCK)
    k5 = k.reshape(B, S, N_HEADS, 2, HEAD_DIM)
    ob = lax.map(lambda t: diff_attend(t[0], k5, v, t[1], pos, lam), (qb, posb))
    return jnp.moveaxis(ob, 0, 1).reshape(B, S, N_HEADS, V_DIM)


def diff_lambda(lq1, lk1, lq2, lk2, li):
    f32 = jnp.float32
    return (jnp.exp(jnp.sum(lq1.astype(f32) * lk1.astype(f32)))
            - jnp.exp(jnp.sum(lq2.astype(f32) * lk2.astype(f32))) + li)


def diff_out(o, subln, li, w_o):
    B, L = o.shape[:2]
    o = rms_norm(o, subln) * (1.0 - li)
    return o.reshape(B, L, N_HEADS * V_DIM) @ w_o


def trunk(x, cache_k, cache_v, state_conv, p):
    sample = cache_k is not None
    B, L, _ = x.shape
    new_k, new_v, new_c = [], [], []
    for i in range(DEPTH):
        x = x + FFN_RES * swiglu(rms_norm(x, p['ffn1_norm'][i]), p['ffn1_w_gate'][i],
                                 p['ffn1_w_up'][i], p['ffn1_w_down'][i])
        h = rms_norm(x, p['mix_norm'][i])
        j = i // N_MIXERS
        if i % N_MIXERS == 0:
            past = state_conv[j] if sample else jnp.zeros((B, CONV_STATE, D_MODEL), h.dtype)
            m, c_new = conv_module(h, past, p['conv_w_pw1'][j], p['conv_b_pw1'][j], p['conv_w_dw'][j],
                                   p['conv_b_dw'][j], p['conv_ln_g'][j], p['conv_ln_b'][j],
                                   p['conv_w_pw2'][j], p['conv_b_pw2'][j])
            new_c.append(c_new)
        else:
            li = lambda_init(i)
            lam = diff_lambda(p['attn_lambda_q1'][j], p['attn_lambda_k1'][j],
                              p['attn_lambda_q2'][j], p['attn_lambda_k2'][j], li)
            q, k, v = diff_qkv(h, p['attn_w_qkv'][j])
            if sample:
                past_len = cache_k.shape[2]
                k_all = jnp.concatenate([cache_k[j].astype(k.dtype), k], axis=1)
                v_all = jnp.concatenate([cache_v[j].astype(v.dtype), v], axis=1)
                q_pos = past_len + jnp.arange(L, dtype=jnp.int32)
                k_pos = jnp.arange(past_len + L, dtype=jnp.int32)
                o = diff_attend(q.reshape(B, L, N_HEADS, 2, HEAD_DIM),
                                k_all.reshape(B, past_len + L, N_HEADS, 2, HEAD_DIM),
                                v_all, q_pos, k_pos, lam)
            else:
                o = diff_attend_blocked(q, k, v, lam)
            m = diff_out(o, p['attn_subln'][j], li, p['attn_w_o'][j])
            new_k.append(k)
            new_v.append(v)
        x = x + m
        x = x + FFN_RES * swiglu(rms_norm(x, p['ffn2_norm'][i]), p['ffn2_w_gate'][i],
                                 p['ffn2_w_up'][i], p['ffn2_w_down'][i])
    y = rms_norm(x, p['final_norm'])
    return y, jnp.stack(new_k), jnp.stack(new_v), jnp.stack(new_c)


def setup_inputs(seed: int = 0) -> dict:
    key = jax.random.key(seed)
    ks = iter(jax.random.split(key, 32))

    def nrm(shape, scale):
        return jax.random.normal(next(ks), shape, jnp.float32) * scale

    D, F = D_MODEL, D_FF
    return {
        'x_prompt': nrm((BATCH, SEQ, D), 1.0),
        'x_sample': nrm((DEC_BATCH, DEC_SEQ, D), 1.0),
        'cache_k': nrm((N_ATTN, DEC_BATCH, PAST_LEN, 2 * N_HEADS, HEAD_DIM), 1.0),
        'cache_v': nrm((N_ATTN, DEC_BATCH, PAST_LEN, N_HEADS, V_DIM), 1.0),
        'state_conv': nrm((N_CONV, DEC_BATCH, CONV_STATE, D), 0.5),
        'ffn1_norm': 1.0 + nrm((DEPTH, D), 0.01),
        'ffn1_w_gate': nrm((DEPTH, D, F), D ** -0.5),
        'ffn1_w_up': nrm((DEPTH, D, F), D ** -0.5),
        'ffn1_w_down': nrm((DEPTH, F, D), F ** -0.5),
        'mix_norm': 1.0 + nrm((DEPTH, D), 0.01),
        'ffn2_norm': 1.0 + nrm((DEPTH, D), 0.01),
        'ffn2_w_gate': nrm((DEPTH, D, F), D ** -0.5),
        'ffn2_w_up': nrm((DEPTH, D, F), D ** -0.5),
        'ffn2_w_down': nrm((DEPTH, F, D), F ** -0.5),
        'conv_w_pw1': nrm((N_CONV, D, 2 * D), D ** -0.5),
        'conv_b_pw1': nrm((N_CONV, 2 * D), 0.01),
        'conv_w_dw': nrm((N_CONV, CONV_WIDTH, D), CONV_WIDTH ** -0.5),
        'conv_b_dw': nrm((N_CONV, D), 0.01),
        'conv_ln_g': 1.0 + nrm((N_CONV, D), 0.01),
        'conv_ln_b': nrm((N_CONV, D), 0.01),
        'conv_w_pw2': nrm((N_CONV, D, D), D ** -0.5),
        'conv_b_pw2': nrm((N_CONV, D), 0.01),
        'attn_w_qkv': nrm((N_ATTN, D, 3 * D), D ** -0.5),
        'attn_lambda_q1': nrm((N_ATTN, HEAD_DIM), 0.1),
        'attn_lambda_k1': nrm((N_ATTN, HEAD_DIM), 0.1),
        'attn_lambda_q2': nrm((N_ATTN, HEAD_DIM), 0.1),
        'attn_lambda_k2': nrm((N_ATTN, HEAD_DIM), 0.1),
        'attn_subln': 1.0 + nrm((N_ATTN, V_DIM), 0.01),
        'attn_w_o': nrm((N_ATTN, N_HEADS * V_DIM, D), (N_HEADS * V_DIM) ** -0.5),
        'final_norm': 1.0 + nrm((D,), 0.01),
    }


def reference(x_prompt, x_sample, cache_k, cache_v, state_conv,
              ffn1_norm, ffn1_w_gate, ffn1_w_up, ffn1_w_down, mix_norm,
              ffn2_norm, ffn2_w_gate, ffn2_w_up, ffn2_w_down,
              conv_w_pw1, conv_b_pw1, conv_w_dw, conv_b_dw, conv_ln_g, conv_ln_b,
              conv_w_pw2, conv_b_pw2,
              attn_w_qkv, attn_lambda_q1, attn_lambda_k1, attn_lambda_q2, attn_lambda_k2,
              attn_subln, attn_w_o, final_norm):
    p = {
        'ffn1_norm': ffn1_norm, 'ffn1_w_gate': ffn1_w_gate, 'ffn1_w_up': ffn1_w_up,
        'ffn1_w_down': ffn1_w_down, 'mix_norm': mix_norm,
        'ffn2_norm': ffn2_norm, 'ffn2_w_gate': ffn2_w_gate, 'ffn2_w_up': ffn2_w_up,
        'ffn2_w_down': ffn2_w_down,
        'conv_w_pw1': conv_w_pw1, 'conv_b_pw1': conv_b_pw1, 'conv_w_dw': conv_w_dw,
        'conv_b_dw': conv_b_dw, 'conv_ln_g': conv_ln_g, 'conv_ln_b': conv_ln_b,
        'conv_w_pw2': conv_w_pw2, 'conv_b_pw2': conv_b_pw2,
        'attn_w_qkv': attn_w_qkv, 'attn_lambda_q1': attn_lambda_q1, 'attn_lambda_k1': attn_lambda_k1,
        'attn_lambda_q2': attn_lambda_q2, 'attn_lambda_k2': attn_lambda_k2,
        'attn_subln': attn_subln, 'attn_w_o': attn_w_o, 'final_norm': final_norm,
    }
    y_prompt, new_k_prompt, new_v_prompt, new_conv_prompt = trunk(x_prompt, None, None, None, p)
    y_sample, new_k_sample, new_v_sample, new_conv_sample = trunk(x_sample, cache_k, cache_v, state_conv, p)
    return (y_prompt, y_sample, new_k_prompt, new_v_prompt, new_conv_prompt,
            new_k_sample, new_v_sample, new_conv_sample)
```

```python
import functools
import math

import jax
import jax.numpy as jnp
from jax import lax
from jax.experimental import pallas as pl
from jax.experimental.pallas import tpu as pltpu

EPS = 1e-6
FFN_RES = 0.5
CHUNK = 64
N_HEADS = 8
HEAD_DIM = 64
V_DIM = 2 * HEAD_DIM
N_MIXERS = 2
NEG_BIG = -1e30

V7X_VMEM_BYTES = 64 * 1024 * 1024
VMEM_LIMIT_BYTES = V7X_VMEM_BYTES - 8 * 1024 * 1024
FFN_F_TILE = 256
CONV_HALO = 32
CONV_ROW_BLOCK = 64
CONV_LANE_BLOCK = 256

bf16 = jnp.bfloat16
f32 = jnp.float32


def _params(n_grid_axes):
    return pltpu.CompilerParams(
        dimension_semantics=("arbitrary",) * n_grid_axes,
        vmem_limit_bytes=VMEM_LIMIT_BYTES)


def _resident(shape):
    nd = len(shape)
    return pl.BlockSpec(shape, lambda *_: (0,) * nd, pipeline_mode=pl.Buffered(1))


def _rms(x, g):
    return x * lax.rsqrt(jnp.mean(x * x, axis=-1, keepdims=True) + EPS) * g


def _silu(x):
    return x * jax.nn.sigmoid(x)


def _token_tile(n_tokens, want):
    t = min(want, n_tokens)
    assert n_tokens % t == 0, (n_tokens, t)
    return t


def _ffn_kernel(*refs, n_f, has_mix, has_final):
    refs = list(refs)
    x_ref = refs.pop(0)
    if has_mix:
        o_ref_in, wo_ref = refs.pop(0), refs.pop(0)
    g_ref, wg_ref, wu_ref, wd_ref = refs[:4]
    refs = refs[4:]
    if has_final:
        gf_ref = refs.pop(0)
    y_ref, h_ref, acc_ref = refs

    x = x_ref[...]
    if has_mix:
        x = x + jnp.dot(o_ref_in[...], wo_ref[...], preferred_element_type=f32)
    h_ref[...] = _rms(x, g_ref[...]).astype(bf16)
    acc_ref[...] = jnp.zeros_like(acc_ref)

    def body(f, carry):
        h = h_ref[...]
        g = jnp.dot(h, wg_ref[f], preferred_element_type=f32)
        u = jnp.dot(h, wu_ref[f], preferred_element_type=f32)
        a = (_silu(g) * u).astype(bf16)
        acc_ref[...] += jnp.dot(a, wd_ref[f], preferred_element_type=f32)
        return carry

    lax.fori_loop(0, n_f, body, 0)
    y = x + FFN_RES * acc_ref[...]
    if has_final:
        y = _rms(y, gf_ref[...])
    y_ref[...] = y


def _ffn(x2, g, wg, wu, wd, mix=None, final_g=None, tm=512):
    t, d = x2.shape
    n_f, _, tf = wg.shape
    tm = _token_tile(t, tm)
    row = pl.BlockSpec((tm, d), lambda i: (i, 0))
    vec = _resident((1, d))
    args, specs = [x2], [row]
    if mix is not None:
        o, wo = mix
        args += [o, wo]
        specs += [pl.BlockSpec((tm, o.shape[1]), lambda i: (i, 0)), _resident(wo.shape)]
    args += [g.reshape(1, d), wg, wu, wd]
    specs += [vec, _resident(wg.shape), _resident(wu.shape), _resident(wd.shape)]
    if final_g is not None:
        args.append(final_g.reshape(1, d))
        specs.append(vec)
    kern = functools.partial(_ffn_kernel, n_f=n_f, has_mix=mix is not None,
                             has_final=final_g is not None)
    return pl.pallas_call(
        kern, out_shape=jax.ShapeDtypeStruct((t, d), f32), grid=(t // tm,),
        in_specs=specs, out_specs=row,
        scratch_shapes=[pltpu.VMEM((tm, d), bf16), pltpu.VMEM((tm, d), f32)],
        compiler_params=_params(1), name="ffn")(*args)


def _conv_kernel(x_ref, past_ref, g_ref, w1_ref, b1_ref, wdw_ref, bdw_ref, lng_ref, lnb_ref,
                 w2_ref, b2_ref, y_ref, ns_ref, ubuf_ref, cbuf_ref, *, tm, d, kw):
    t = pl.program_id(1)
    n_past = kw - 1
    off = CONV_HALO - n_past

    @pl.when(t == 0)
    def _():
        ubuf_ref[0:CONV_HALO, :] = jnp.zeros((CONV_HALO, d), f32)
        ubuf_ref[off:CONV_HALO, :] = past_ref[0]

    x = x_ref[0]
    h = _rms(x, g_ref[...]).astype(bf16)
    a = jnp.dot(h, w1_ref[...], preferred_element_type=f32) + b1_ref[...]
    ubuf_ref[CONV_HALO:CONV_HALO + tm, :] = a[:, :d] * jax.nn.sigmoid(a[:, d:])

    rb = min(CONV_ROW_BLOCK, tm)
    for r0 in range(0, tm, rb):
        for c0 in range(0, d, CONV_LANE_BLOCK):
            cs = slice(c0, c0 + CONV_LANE_BLOCK)
            acc = jnp.broadcast_to(bdw_ref[:, cs], (rb, CONV_LANE_BLOCK))
            for k in range(kw):
                acc = acc + wdw_ref[k:k + 1, cs] * ubuf_ref[off + r0 + k:off + r0 + k + rb, cs]
            cbuf_ref[r0:r0 + rb, cs] = acc

    c = cbuf_ref[...]
    xc = c - jnp.mean(c, axis=-1, keepdims=True)
    ln = xc * lax.rsqrt(jnp.mean(xc * xc, axis=-1, keepdims=True) + EPS) * lng_ref[...] + lnb_ref[...]
    m = jnp.dot(_silu(ln).astype(bf16), w2_ref[...], preferred_element_type=f32) + b2_ref[...]
    y_ref[0] = x + m

    ns_ref[0] = ubuf_ref[tm + off:tm + CONV_HALO, :]
    ubuf_ref[0:CONV_HALO, :] = ubuf_ref[tm:tm + CONV_HALO, :]


def _conv_mixer(x, past, g, w1, b1, wdw, bdw, lng, lnb, w2, b2, tm=256):
    b, l, d = x.shape
    kw = wdw.shape[0]
    assert kw - 1 <= CONV_HALO
    tm = _token_tile(l, tm)
    assert tm >= CONV_HALO and tm % 8 == 0
    tile = pl.BlockSpec((1, tm, d), lambda i, j: (i, j, 0))
    state = pl.BlockSpec((1, kw - 1, d), lambda i, j: (i, 0, 0))
    vec = _resident((1, d))
    kern = functools.partial(_conv_kernel, tm=tm, d=d, kw=kw)
    return pl.pallas_call(
        kern,
        out_shape=(jax.ShapeDtypeStruct((b, l, d), f32), jax.ShapeDtypeStruct((b, kw - 1, d), f32)),
        grid=(b, l // tm),
        in_specs=[tile, state, vec, _resident(w1.shape), _resident((1, 2 * d)), _resident(wdw.shape),
                  vec, vec, vec, _resident(w2.shape), vec],
        out_specs=(tile, state),
        scratch_shapes=[pltpu.VMEM((CONV_HALO + tm, d), f32), pltpu.VMEM((tm, d), f32)],
        compiler_params=_params(2), name="conv_mixer",
    )(x, past, g.reshape(1, d), w1, b1.reshape(1, 2 * d), wdw, bdw.reshape(1, d),
      lng.reshape(1, d), lnb.reshape(1, d), w2, b2.reshape(1, d))


def _qkv_kernel(x_ref, g_ref, w_ref, q_ref, k_ref, v_ref, kb_ref, vb_ref, *, d):
    h = _rms(x_ref[...], g_ref[...]).astype(bf16)
    q = jnp.dot(h, w_ref[:, 0:d], preferred_element_type=f32)
    q_ref[...] = (q * (HEAD_DIM ** -0.5)).astype(bf16)
    k = jnp.dot(h, w_ref[:, d:2 * d], preferred_element_type=f32)
    k_ref[...] = k
    kb_ref[...] = k.astype(bf16)
    v = jnp.dot(h, w_ref[:, 2 * d:3 * d], preferred_element_type=f32)
    v_ref[...] = v
    vb_ref[...] = v.astype(bf16)


def _qkv(x2, g, w, tm=512):
    t, d = x2.shape
    tm = _token_tile(t, tm)
    row = pl.BlockSpec((tm, d), lambda i: (i, 0))
    return pl.pallas_call(
        functools.partial(_qkv_kernel, d=d),
        out_shape=(jax.ShapeDtypeStruct((t, d), bf16), jax.ShapeDtypeStruct((t, d), f32),
                   jax.ShapeDtypeStruct((t, d), f32), jax.ShapeDtypeStruct((t, d), bf16),
                   jax.ShapeDtypeStruct((t, d), bf16)),
        grid=(t // tm,),
        in_specs=[row, _resident((1, d)), _resident(w.shape)],
        out_specs=(row, row, row, row, row),
        compiler_params=_params(1), name="qkv")(x2, g.reshape(1, d), w)


def _chunk_of(pos):
    assert CHUNK & (CHUNK - 1) == 0
    return lax.shift_right_logical(pos, CHUNK.bit_length() - 1)


def _attn_kernel(*refs, tq, tk, tp, past_len, n_new, li):
    refs = list(refs)
    lam_ref, sub_ref, q_ref, k_ref, v_ref = refs[:5]
    refs = refs[5:]
    if past_len:
        pk_ref, pv_ref = refs[:2]
        refs = refs[2:]
    o_ref, qs_ref, m_ref, l_ref, acc_ref = refs
    qi = pl.program_id(2)
    q_pos0 = past_len + qi * tq

    q = q_ref[0]
    lane = lax.broadcasted_iota(jnp.int32, q.shape, 1)
    qs_ref[0:tq, :] = jnp.where(lane < HEAD_DIM, q, jnp.zeros_like(q))
    qs_ref[tq:2 * tq, :] = jnp.where(lane >= HEAD_DIM, q, jnp.zeros_like(q))
    m_ref[...] = jnp.full_like(m_ref, NEG_BIG)
    l_ref[...] = jnp.zeros_like(l_ref)
    acc_ref[...] = jnp.zeros_like(acc_ref)

    def update(kt, vt, k_pos0):
        s = lax.dot_general(qs_ref[...], kt, (((1,), (1,)), ((), ())), preferred_element_type=f32)
        if k_pos0 is not None:
            row = lax.broadcasted_iota(jnp.int32, s.shape, 0)
            qp = q_pos0 + jnp.where(row >= tq, row - tq, row)
            kp = k_pos0 + lax.broadcasted_iota(jnp.int32, s.shape, 1)
            s = jnp.where(_chunk_of(kp) <= _chunk_of(qp), s, NEG_BIG)
        m_prev = m_ref[...]
        m_new = jnp.maximum(m_prev, jnp.max(s, axis=-1, keepdims=True))
        alpha = jnp.exp(m_prev - m_new)
        p = jnp.exp(s - m_new)
        l_ref[...] = alpha * l_ref[...] + jnp.sum(p, axis=-1, keepdims=True)
        acc_ref[...] = alpha * acc_ref[...] + jnp.dot(p.astype(bf16), vt, preferred_element_type=f32)
        m_ref[...] = m_new

    if past_len:
        def past_body(j, carry):
            rows = pl.ds(pl.multiple_of(j * tp, tp), tp)
            update(pk_ref[0, rows, :].astype(bf16), pv_ref[0, rows, :].astype(bf16), None)
            return carry
        lax.fori_loop(0, past_len // tp, past_body, 0)

    first_end = (q_pos0 // CHUNK + 1) * CHUNK
    last_end = ((q_pos0 + tq - 1) // CHUNK + 1) * CHUNK
    n_full = jnp.clip((first_end - past_len) // tk, 0, n_new)
    n_need = jnp.clip((last_end - past_len + tk - 1) // tk, 0, n_new)

    def new_body(masked, j, carry):
        rows = pl.ds(pl.multiple_of(j * tk, tk), tk)
        update(k_ref[0, rows, :], v_ref[0, rows, :], past_len + j * tk if masked else None)
        return carry

    lax.fori_loop(0, n_full, functools.partial(new_body, False), 0)
    lax.fori_loop(n_full, n_need, functools.partial(new_body, True), 0)

    lam_rows = lam_ref[...]
    lam = (jnp.exp(jnp.sum(lam_rows[0:1] * lam_rows[1:2], axis=-1, keepdims=True))
           - jnp.exp(jnp.sum(lam_rows[2:3] * lam_rows[3:4], axis=-1, keepdims=True)) + li)
    o_all = acc_ref[...] / l_ref[...]
    o = o_all[0:tq] - lam * o_all[tq:2 * tq]
    o_ref[0] = (_rms(o, sub_ref[...]) * (1.0 - li)).astype(bf16)


def _diff_attention(q, k, v, lam_rows, subln, li, past=None, past_row0=0, tq=512, tk=256, tp=256):
    b, l, d = q.shape
    tq = _token_tile(l, tq)
    tk = _token_tile(l, tk)
    past_len = 0 if past is None else past[0].shape[1]
    if past_len:
        tp = _token_tile(past_len, tp)
    qtile = pl.BlockSpec((1, tq, V_DIM), lambda i, h, j: (i, j, h))
    seq = pl.BlockSpec((1, l, V_DIM), lambda i, h, j: (i, 0, h))
    args = [lam_rows, subln.reshape(1, V_DIM), q, k, v]
    specs = [_resident(lam_rows.shape), _resident((1, V_DIM)), qtile, seq, seq]
    if past_len:
        pseq = pl.BlockSpec((1, past_len, V_DIM), lambda i, h, j: (past_row0 + i, 0, h))
        args += list(past)
        specs += [pseq, pseq]
    kern = functools.partial(_attn_kernel, tq=tq, tk=tk, tp=tp, past_len=past_len,
                             n_new=l // tk, li=li)
    return pl.pallas_call(
        kern, out_shape=jax.ShapeDtypeStruct((b, l, d), bf16), grid=(b, d // V_DIM, l // tq),
        in_specs=specs, out_specs=qtile,
        scratch_shapes=[pltpu.VMEM((2 * tq, V_DIM), bf16), pltpu.VMEM((2 * tq, 1), f32),
                        pltpu.VMEM((2 * tq, 1), f32), pltpu.VMEM((2 * tq, V_DIM), f32)],
        compiler_params=_params(3), name="diff_attn")(*args)


def _lambda_init(layer):
    return 0.8 - 0.6 * math.exp(-0.3 * layer)


def _trunk(x, cache_k, cache_v, state_conv, w):
    b, l, d = x.shape
    depth = w['ffn1_norm'].shape[0]
    x2 = x.reshape(b * l, d)
    new_k, new_v, new_c = [], [], []
    for i in range(depth):
        j = i // N_MIXERS
        x2 = _ffn(x2, w['ffn1_norm'][i], *w['ffn1'][i])
        mix = None
        if i % N_MIXERS == 0:
            past = (jnp.zeros((b, w['conv_w_dw'].shape[1] - 1, d), f32) if state_conv is None
                    else state_conv[j])
            y, c_new = _conv_mixer(x2.reshape(b, l, d), past, w['mix_norm'][i], w['conv_w_pw1'][j],
                                   w['conv_b_pw1'][j], w['conv_w_dw'][j], w['conv_b_dw'][j],
                                   w['conv_ln_g'][j], w['conv_ln_b'][j], w['conv_w_pw2'][j],
                                   w['conv_b_pw2'][j])
            x2 = y.reshape(b * l, d)
            new_c.append(c_new)
        else:
            q, k, v, kb, vb = _qkv(x2, w['mix_norm'][i], w['attn_w_qkv'][j])
            past = None
            if cache_k is not None:
                n_attn, _, p_len = cache_k.shape[:3]
                past = (cache_k.reshape(n_attn * b, p_len, d), cache_v.reshape(n_attn * b, p_len, d))
            o = _diff_attention(q.reshape(b, l, d), kb.reshape(b, l, d), vb.reshape(b, l, d),
                                w['attn_lambda'][j], w['attn_subln'][j], _lambda_init(i), past, j * b)
            mix = (o.reshape(b * l, d), w['attn_w_o'][j])
            new_k.append(k.reshape(b, l, 2 * N_HEADS, HEAD_DIM))
            new_v.append(v.reshape(b, l, N_HEADS, V_DIM))
        x2 = _ffn(x2, w['ffn2_norm'][i], *w['ffn2'][i], mix=mix,
                  final_g=w['final_norm'] if i == depth - 1 else None)
    return x2.reshape(b, l, d), jnp.stack(new_k), jnp.stack(new_v), jnp.stack(new_c)


def _ffn_weights(w_gate, w_up, w_down):
    d, f = w_gate.shape
    n_f = f // FFN_F_TILE
    assert n_f * FFN_F_TILE == f
    split = lambda m: m.astype(bf16).reshape(d, n_f, FFN_F_TILE).transpose(1, 0, 2)
    return split(w_gate), split(w_up), w_down.astype(bf16).reshape(n_f, FFN_F_TILE, d)


def kernel(x_prompt, x_sample, cache_k, cache_v, state_conv, ffn1_norm, ffn1_w_gate, ffn1_w_up, ffn1_w_down, mix_norm, ffn2_norm, ffn2_w_gate, ffn2_w_up, ffn2_w_down, conv_w_pw1, conv_b_pw1, conv_w_dw, conv_b_dw, conv_ln_g, conv_ln_b, conv_w_pw2, conv_b_pw2, attn_w_qkv, attn_lambda_q1, attn_lambda_k1, attn_lambda_q2, attn_lambda_k2, attn_subln, attn_w_o, final_norm):
    depth = ffn1_norm.shape[0]
    w = {
        'ffn1_norm': ffn1_norm, 'mix_norm': mix_norm, 'ffn2_norm': ffn2_norm, 'final_norm': final_norm,
        'ffn1': [_ffn_weights(ffn1_w_gate[i], ffn1_w_up[i], ffn1_w_down[i]) for i in range(depth)],
        'ffn2': [_ffn_weights(ffn2_w_gate[i], ffn2_w_up[i], ffn2_w_down[i]) for i in range(depth)],
        'conv_w_pw1': conv_w_pw1.astype(bf16), 'conv_b_pw1': conv_b_pw1, 'conv_w_dw': conv_w_dw,
        'conv_b_dw': conv_b_dw, 'conv_ln_g': conv_ln_g, 'conv_ln_b': conv_ln_b,
        'conv_w_pw2': conv_w_pw2.astype(bf16), 'conv_b_pw2': conv_b_pw2,
        'attn_w_qkv': attn_w_qkv.astype(bf16), 'attn_w_o': attn_w_o.astype(bf16),
        'attn_subln': attn_subln,
        'attn_lambda': jnp.stack([attn_lambda_q1, attn_lambda_k1, attn_lambda_q2, attn_lambda_k2], axis=1),
    }
    y_p, k_p, v_p, c_p = _trunk(x_prompt, None, None, None, w)
    y_s, k_s, v_s, c_s = _trunk(x_sample, cache_k, cache_v, state_conv, w)
    return (y_p, y_s, k_p, v_p, c_p, k_s, v_s, c_s)
```

```python
import functools
import math

import jax
import jax.numpy as jnp
from jax import lax
from jax.experimental import pallas as pl
from jax.experimental.pallas import tpu as pltpu

EPS = 1e-6
FFN_RES = 0.5
CHUNK = 64
N_HEADS = 8
HEAD_DIM = 64
V_DIM = 2 * HEAD_DIM
N_MIXERS = 2
NEG_BIG = -1e30

V7X_VMEM_BYTES = 64 * 1024 * 1024
VMEM_LIMIT_BYTES = V7X_VMEM_BYTES - 8 * 1024 * 1024
FFN_F_TILE = 256
CONV_HALO = 32
CONV_ROW_BLOCK = 64
CONV_LANE_BLOCK = 256

bf16 = jnp.bfloat16
f32 = jnp.float32


def _params(n_grid_axes):
    return pltpu.CompilerParams(
        dimension_semantics=("arbitrary",) * n_grid_axes,
        vmem_limit_bytes=VMEM_LIMIT_BYTES)


def _resident(shape):
    nd = len(shape)
    return pl.BlockSpec(shape, lambda *_: (0,) * nd, pipeline_mode=pl.Buffered(1))


def _rms(x, g):
    return x * lax.rsqrt(jnp.mean(x * x, axis=-1, keepdims=True) + EPS) * g


def _silu(x):
    return x * jax.nn.sigmoid(x)


def _token_tile(n_tokens, want):
    t = min(want, n_tokens)
    assert n_tokens % t == 0, (n_tokens, t)
    return t


def _ffn_kernel(*refs, n_f, has_mix, has_final):
    refs = list(refs)
    x_ref = refs.pop(0)
    if has_mix:
        o_ref_in, wo_ref = refs.pop(0), refs.pop(0)
    g_ref, wg_ref, wu_ref, wd_ref = refs[:4]
    refs = refs[4:]
    if has_final:
        gf_ref = refs.pop(0)
    y_ref, h_ref, acc_ref = refs

    x = x_ref[...]
    if has_mix:
        x = x + jnp.dot(o_ref_in[...], wo_ref[...], preferred_element_type=f32)
    h_ref[...] = _rms(x, g_ref[...]).astype(bf16)

    for f in range(n_f):
        cs = slice(f * FFN_F_TILE, (f + 1) * FFN_F_TILE)
        h = h_ref[...]
        g = jnp.dot(h, wg_ref[0, :, cs], preferred_element_type=f32)
        u = jnp.dot(h, wu_ref[0, :, cs], preferred_element_type=f32)
        a = (_silu(g) * u).astype(bf16)
        part = jnp.dot(a, wd_ref[0, cs, :], preferred_element_type=f32)
        if f == 0:
            acc_ref[...] = part
        else:
            acc_ref[...] += part
    y = x + FFN_RES * acc_ref[...]
    if has_final:
        y = _rms(y, gf_ref[...])
    y_ref[...] = y


def _layer_resident(stack, layer):
    nd = stack.ndim
    return pl.BlockSpec((1,) + stack.shape[1:], lambda *_: (layer,) + (0,) * (nd - 1),
                        pipeline_mode=pl.Buffered(1))


def _ffn(x2, g, wg, wu, wd, layer, mix=None, final_g=None, tm=512):
    t, d = x2.shape
    f_dim = wg.shape[2]
    n_f = f_dim // FFN_F_TILE
    assert n_f * FFN_F_TILE == f_dim
    tm = _token_tile(t, tm)
    row = pl.BlockSpec((tm, d), lambda i: (i, 0))
    vec = _resident((1, d))
    args, specs = [x2], [row]
    if mix is not None:
        o, wo = mix
        args += [o, wo]
        specs += [pl.BlockSpec((tm, o.shape[1]), lambda i: (i, 0)), _resident(wo.shape)]
    args += [g.reshape(1, d), wg, wu, wd]
    specs += [vec, _layer_resident(wg, layer), _layer_resident(wu, layer), _layer_resident(wd, layer)]
    if final_g is not None:
        args.append(final_g.reshape(1, d))
        specs.append(vec)
    kern = functools.partial(_ffn_kernel, n_f=n_f, has_mix=mix is not None,
                             has_final=final_g is not None)
    return pl.pallas_call(
        kern, out_shape=jax.ShapeDtypeStruct((t, d), f32), grid=(t // tm,),
        in_specs=specs, out_specs=row,
        scratch_shapes=[pltpu.VMEM((tm, d), bf16), pltpu.VMEM((tm, d), f32)],
        compiler_params=_params(1), name="ffn")(*args)


def _conv_kernel(x_ref, past_ref, g_ref, w1_ref, b1_ref, wdw_ref, bdw_ref, lng_ref, lnb_ref,
                 w2_ref, b2_ref, y_ref, ns_ref, ubuf_ref, cbuf_ref, *, tm, d, kw):
    t = pl.program_id(1)
    n_past = kw - 1
    off = CONV_HALO - n_past

    @pl.when(t == 0)
    def _():
        ubuf_ref[0:CONV_HALO, :] = jnp.zeros((CONV_HALO, d), f32)
        ubuf_ref[off:CONV_HALO, :] = past_ref[0]

    x = x_ref[0]
    h = _rms(x, g_ref[...]).astype(bf16)
    a = jnp.dot(h, w1_ref[...], preferred_element_type=f32) + b1_ref[...]
    ubuf_ref[CONV_HALO:CONV_HALO + tm, :] = a[:, :d] * jax.nn.sigmoid(a[:, d:])

    rb = min(CONV_ROW_BLOCK, tm)
    for r0 in range(0, tm, rb):
        for c0 in range(0, d, CONV_LANE_BLOCK):
            cs = slice(c0, c0 + CONV_LANE_BLOCK)
            acc = jnp.broadcast_to(bdw_ref[:, cs], (rb, CONV_LANE_BLOCK))
            for k in range(kw):
                acc = acc + wdw_ref[k:k + 1, cs] * ubuf_ref[off + r0 + k:off + r0 + k + rb, cs]
            cbuf_ref[r0:r0 + rb, cs] = acc

    c = cbuf_ref[...]
    xc = c - jnp.mean(c, axis=-1, keepdims=True)
    ln = xc * lax.rsqrt(jnp.mean(xc * xc, axis=-1, keepdims=True) + EPS) * lng_ref[...] + lnb_ref[...]
    m = jnp.dot(_silu(ln).astype(bf16), w2_ref[...], preferred_element_type=f32) + b2_ref[...]
    y_ref[0] = x + m

    ns_ref[0] = ubuf_ref[tm + off:tm + CONV_HALO, :]
    ubuf_ref[0:CONV_HALO, :] = ubuf_ref[tm:tm + CONV_HALO, :]


def _conv_mixer(x, past, g, w1, b1, wdw, bdw, lng, lnb, w2, b2, tm=256):
    b, l, d = x.shape
    kw = wdw.shape[0]
    assert kw - 1 <= CONV_HALO
    tm = _token_tile(l, tm)
    assert tm >= CONV_HALO and tm % 8 == 0
    tile = pl.BlockSpec((1, tm, d), lambda i, j: (i, j, 0))
    state = pl.BlockSpec((1, kw - 1, d), lambda i, j: (i, 0, 0))
    vec = _resident((1, d))
    kern = functools.partial(_conv_kernel, tm=tm, d=d, kw=kw)
    return pl.pallas_call(
        kern,
        out_shape=(jax.ShapeDtypeStruct((b, l, d), f32), jax.ShapeDtypeStruct((b, kw - 1, d), f32)),
        grid=(b, l // tm),
        in_specs=[tile, state, vec, _resident(w1.shape), _resident((1, 2 * d)), _resident(wdw.shape),
                  vec, vec, vec, _resident(w2.shape), vec],
        out_specs=(tile, state),
        scratch_shapes=[pltpu.VMEM((CONV_HALO + tm, d), f32), pltpu.VMEM((tm, d), f32)],
        compiler_params=_params(2), name="conv_mixer",
    )(x, past, g.reshape(1, d), w1, b1.reshape(1, 2 * d), wdw, bdw.reshape(1, d),
      lng.reshape(1, d), lnb.reshape(1, d), w2, b2.reshape(1, d))


def _qkv_kernel(x_ref, g_ref, w_ref, q_ref, k_ref, v_ref, kb_ref, vb_ref, *, d):
    h = _rms(x_ref[...], g_ref[...]).astype(bf16)
    q = jnp.dot(h, w_ref[:, 0:d], preferred_element_type=f32)
    q_ref[...] = (q * (HEAD_DIM ** -0.5)).astype(bf16)
    k = jnp.dot(h, w_ref[:, d:2 * d], preferred_element_type=f32)
    k_ref[...] = k
    kb_ref[...] = k.astype(bf16)
    v = jnp.dot(h, w_ref[:, 2 * d:3 * d], preferred_element_type=f32)
    v_ref[...] = v
    vb_ref[...] = v.astype(bf16)


def _qkv(x2, g, w, tm=512):
    t, d = x2.shape
    tm = _token_tile(t, tm)
    row = pl.BlockSpec((tm, d), lambda i: (i, 0))
    return pl.pallas_call(
        functools.partial(_qkv_kernel, d=d),
        out_shape=(jax.ShapeDtypeStruct((t, d), bf16), jax.ShapeDtypeStruct((t, d), f32),
                   jax.ShapeDtypeStruct((t, d), f32), jax.ShapeDtypeStruct((t, d), bf16),
                   jax.ShapeDtypeStruct((t, d), bf16)),
        grid=(t // tm,),
        in_specs=[row, _resident((1, d)), _resident(w.shape)],
        out_specs=(row, row, row, row, row),
        compiler_params=_params(1), name="qkv")(x2, g.reshape(1, d), w)


def _chunk_of(pos):
    assert CHUNK & (CHUNK - 1) == 0
    return lax.shift_right_logical(pos, CHUNK.bit_length() - 1)


def _attn_kernel(*refs, hp, tq, tk, tp, past_len, n_new, li):
    refs = list(refs)
    lam_ref, sub_ref, q_ref, k_ref, v_ref = refs[:5]
    refs = refs[5:]
    if past_len:
        pk_ref, pv_ref = refs[:2]
        refs = refs[2:]
    o_ref, qs_ref, m_ref, l_ref, acc_ref = refs
    qi = pl.program_id(2)
    q_pos0 = past_len + qi * tq
    heads = [slice(h * V_DIM, (h + 1) * V_DIM) for h in range(hp)]

    for h, hs in enumerate(heads):
        q = q_ref[0, :, hs]
        lane = lax.broadcasted_iota(jnp.int32, q.shape, 1)
        qs_ref[h, 0:tq, :] = jnp.where(lane < HEAD_DIM, q, jnp.zeros_like(q))
        qs_ref[h, tq:2 * tq, :] = jnp.where(lane >= HEAD_DIM, q, jnp.zeros_like(q))
    m_ref[...] = jnp.full_like(m_ref, NEG_BIG)
    l_ref[...] = jnp.zeros_like(l_ref)
    acc_ref[...] = jnp.zeros_like(acc_ref)

    def update(kv_of_head, n, k_pos0):
        ones = jnp.ones((n, V_DIM), bf16)
        for h in range(hp):
            kt, vt = kv_of_head(h)
            s = lax.dot_general(qs_ref[h], kt, (((1,), (1,)), ((), ())), preferred_element_type=f32)
            if k_pos0 is not None:
                row = lax.broadcasted_iota(jnp.int32, s.shape, 0)
                qp = q_pos0 + jnp.where(row >= tq, row - tq, row)
                kp = k_pos0 + lax.broadcasted_iota(jnp.int32, s.shape, 1)
                s = jnp.where(_chunk_of(kp) <= _chunk_of(qp), s, NEG_BIG)
            m_prev = m_ref[h]
            m_new = jnp.maximum(m_prev, jnp.max(s, axis=-1, keepdims=True))
            alpha = jnp.exp(m_prev - m_new)
            if n % V_DIM == 0:
                m_wide = m_new if n == V_DIM else jnp.concatenate([m_new] * (n // V_DIM), axis=1)
            else:
                assert n < V_DIM
                m_wide = m_new[:, :n]
            p = jnp.exp(s - m_wide).astype(bf16)
            pv = jnp.dot(p, jnp.concatenate([vt, ones], axis=1), preferred_element_type=f32)
            acc_ref[h] = alpha * acc_ref[h] + pv[:, :V_DIM]
            l_ref[h] = alpha * l_ref[h] + pv[:, V_DIM:]
            m_ref[h] = m_new

    if past_len:
        def past_body(j, carry):
            rows = pl.ds(pl.multiple_of(j * tp, tp), tp)
            update(lambda h: (pk_ref[0, rows, heads[h]].astype(bf16),
                              pv_ref[0, rows, heads[h]].astype(bf16)), tp, None)
            return carry
        lax.fori_loop(0, past_len // tp, past_body, 0)

    first_end = (q_pos0 // CHUNK + 1) * CHUNK
    last_end = ((q_pos0 + tq - 1) // CHUNK + 1) * CHUNK
    n_full = jnp.clip((first_end - past_len) // tk, 0, n_new)
    n_need = jnp.clip((last_end - past_len + tk - 1) // tk, 0, n_new)

    def new_body(masked, j, carry):
        rows = pl.ds(pl.multiple_of(j * tk, tk), tk)
        update(lambda h: (k_ref[0, rows, heads[h]], v_ref[0, rows, heads[h]]), tk,
               past_len + j * tk if masked else None)
        return carry

    lax.fori_loop(0, n_full, functools.partial(new_body, False), 0)
    lax.fori_loop(n_full, n_need, functools.partial(new_body, True), 0)

    lam_rows = lam_ref[...]
    lam = (jnp.exp(jnp.sum(lam_rows[0:1] * lam_rows[1:2], axis=-1, keepdims=True))
           - jnp.exp(jnp.sum(lam_rows[2:3] * lam_rows[3:4], axis=-1, keepdims=True)) + li)
    for h, hs in enumerate(heads):
        o_all = acc_ref[h] / l_ref[h]
        o = o_all[0:tq] - lam * o_all[tq:2 * tq]
        o_ref[0, :, hs] = (_rms(o, sub_ref[...]) * (1.0 - li)).astype(bf16)


def _diff_attention(q, k, v, lam_rows, subln, li, past=None, past_row0=0, hp=2, tq=512, tk=512, tp=512):
    b, l, d = q.shape
    n_heads = d // V_DIM
    assert n_heads % hp == 0
    w = hp * V_DIM
    tq = _token_tile(l, tq)
    tk = _token_tile(l, tk)
    past_len = 0 if past is None else past[0].shape[1]
    if past_len:
        tp = _token_tile(past_len, tp)
    qtile = pl.BlockSpec((1, tq, w), lambda i, g, j: (i, j, g))
    seq = pl.BlockSpec((1, l, w), lambda i, g, j: (i, 0, g))
    args = [lam_rows, subln.reshape(1, V_DIM), q, k, v]
    specs = [_resident(lam_rows.shape), _resident((1, V_DIM)), qtile, seq, seq]
    if past_len:
        pseq = pl.BlockSpec((1, past_len, w), lambda i, g, j: (past_row0 + i, 0, g))
        args += list(past)
        specs += [pseq, pseq]
    kern = functools.partial(_attn_kernel, hp=hp, tq=tq, tk=tk, tp=tp, past_len=past_len,
                             n_new=l // tk, li=li)
    stat = pltpu.VMEM((hp, 2 * tq, V_DIM), f32)
    return pl.pallas_call(
        kern, out_shape=jax.ShapeDtypeStruct((b, l, d), bf16), grid=(b, n_heads // hp, l // tq),
        in_specs=specs, out_specs=qtile,
        scratch_shapes=[pltpu.VMEM((hp, 2 * tq, V_DIM), bf16), stat, stat, stat],
        compiler_params=_params(3), name="diff_attn")(*args)


def _lambda_init(layer):
    return 0.8 - 0.6 * math.exp(-0.3 * layer)


def _trunk(x, cache_k, cache_v, state_conv, w):
    b, l, d = x.shape
    depth = w['ffn1_norm'].shape[0]
    x2 = x.reshape(b * l, d)
    new_k, new_v, new_c = [], [], []
    for i in range(depth):
        j = i // N_MIXERS
        x2 = _ffn(x2, w['ffn1_norm'][i], *w['ffn1'], i)
        mix = None
        if i % N_MIXERS == 0:
            past = (jnp.zeros((b, w['conv_w_dw'].shape[1] - 1, d), f32) if state_conv is None
                    else state_conv[j])
            y, c_new = _conv_mixer(x2.reshape(b, l, d), past, w['mix_norm'][i], w['conv_w_pw1'][j],
                                   w['conv_b_pw1'][j], w['conv_w_dw'][j], w['conv_b_dw'][j],
                                   w['conv_ln_g'][j], w['conv_ln_b'][j], w['conv_w_pw2'][j],
                                   w['conv_b_pw2'][j])
            x2 = y.reshape(b * l, d)
            new_c.append(c_new)
        else:
            q, k, v, kb, vb = _qkv(x2, w['mix_norm'][i], w['attn_w_qkv'][j])
            past, cfg = None, dict(hp=2)
            if cache_k is not None:
                n_attn, _, p_len = cache_k.shape[:3]
                past = (cache_k.reshape(n_attn * b, p_len, d), cache_v.reshape(n_attn * b, p_len, d))
                cfg = dict(hp=N_HEADS, tp=p_len, past_row0=j * b)
            o = _diff_attention(q.reshape(b, l, d), kb.reshape(b, l, d), vb.reshape(b, l, d),
                                w['attn_lambda'][j], w['attn_subln'][j], _lambda_init(i), past, **cfg)
            mix = (o.reshape(b * l, d), w['attn_w_o'][j])
            new_k.append(k.reshape(b, l, 2 * N_HEADS, HEAD_DIM))
            new_v.append(v.reshape(b, l, N_HEADS, V_DIM))
        x2 = _ffn(x2, w['ffn2_norm'][i], *w['ffn2'], i, mix=mix,
                  final_g=w['final_norm'] if i == depth - 1 else None)
    return x2.reshape(b, l, d), jnp.stack(new_k), jnp.stack(new_v), jnp.stack(new_c)


def kernel(x_prompt, x_sample, cache_k, cache_v, state_conv, ffn1_norm, ffn1_w_gate, ffn1_w_up, ffn1_w_down, mix_norm, ffn2_norm, ffn2_w_gate, ffn2_w_up, ffn2_w_down, conv_w_pw1, conv_b_pw1, conv_w_dw, conv_b_dw, conv_ln_g, conv_ln_b, conv_w_pw2, conv_b_pw2, attn_w_qkv, attn_lambda_q1, attn_lambda_k1, attn_lambda_q2, attn_lambda_k2, attn_subln, attn_w_o, final_norm):
    depth = ffn1_norm.shape[0]
    w = {
        'ffn1_norm': ffn1_norm, 'mix_norm': mix_norm, 'ffn2_norm': ffn2_norm, 'final_norm': final_norm,
        'ffn1': (ffn1_w_gate.astype(bf16), ffn1_w_up.astype(bf16), ffn1_w_down.astype(bf16)),
        'ffn2': (ffn2_w_gate.astype(bf16), ffn2_w_up.astype(bf16), ffn2_w_down.astype(bf16)),
        'conv_w_pw1': conv_w_pw1.astype(bf16), 'conv_b_pw1': conv_b_pw1, 'conv_w_dw': conv_w_dw,
        'conv_b_dw': conv_b_dw, 'conv_ln_g': conv_ln_g, 'conv_ln_b': conv_ln_b,
        'conv_w_pw2': conv_w_pw2.astype(bf16), 'conv_b_pw2': conv_b_pw2,
        'attn_w_qkv': attn_w_qkv.astype(bf16), 'attn_w_o': attn_w_o.astype(bf16),
        'attn_subln': attn_subln,
        'attn_lambda': jnp.stack([attn_lambda_q1, attn_lambda_k1, attn_lambda_q2, attn_lambda_k2], axis=1),
    }
    y_p, k_p, v_p, c_p = _trunk(x_prompt, None, None, None, w)
    y_s, k_s, v_s, c_s = _trunk(x_sample, cache_k, cache_v, state_conv, w)
    return (y_p, y_s, k_p, v_p, c_p, k_s, v_s, c_s)
```

```python
import functools
import math

import jax
import jax.numpy as jnp
from jax import lax
from jax.experimental import pallas as pl
from jax.experimental.pallas import tpu as pltpu

EPS = 1e-6
FFN_RES = 0.5
CHUNK = 64
N_HEADS = 8
HEAD_DIM = 64
V_DIM = 2 * HEAD_DIM
N_MIXERS = 2
NEG_BIG = -1e30

V7X_VMEM_BYTES = 64 * 1024 * 1024
VMEM_LIMIT_BYTES = V7X_VMEM_BYTES - 8 * 1024 * 1024
FFN_F_TILE = 256
CONV_HALO = 32
CONV_ROW_BLOCK = 64
CONV_LANE_BLOCK = 128
SUBLANES = 8

bf16 = jnp.bfloat16
f32 = jnp.float32


def _params(n_grid_axes):
    return pltpu.CompilerParams(
        dimension_semantics=("arbitrary",) * n_grid_axes,
        vmem_limit_bytes=VMEM_LIMIT_BYTES)


def _resident(shape):
    nd = len(shape)
    return pl.BlockSpec(shape, lambda *_: (0,) * nd, pipeline_mode=pl.Buffered(1))


def _rms(x, g):
    return x * lax.rsqrt(jnp.mean(x * x, axis=-1, keepdims=True) + EPS) * g


def _silu(x):
    return x * jax.nn.sigmoid(x)


def _token_tile(n_tokens, want):
    t = min(want, n_tokens)
    assert n_tokens % t == 0, (n_tokens, t)
    return t


def _ffn_kernel(*refs, n_f, has_mix, has_final):
    refs = list(refs)
    x_ref = refs.pop(0)
    if has_mix:
        o_ref_in, wo_ref = refs.pop(0), refs.pop(0)
    g_ref, wg_ref, wu_ref, wd_ref = refs[:4]
    refs = refs[4:]
    if has_final:
        gf_ref = refs.pop(0)
    y_ref, h_ref, acc_ref = refs

    x = x_ref[...]
    if has_mix:
        x = x + jnp.dot(o_ref_in[...], wo_ref[...], preferred_element_type=f32)
    h_ref[...] = _rms(x, g_ref[...]).astype(bf16)

    for f in range(n_f):
        cs = slice(f * FFN_F_TILE, (f + 1) * FFN_F_TILE)
        h = h_ref[...]
        g = jnp.dot(h, wg_ref[0, :, cs], preferred_element_type=f32)
        u = jnp.dot(h, wu_ref[0, :, cs], preferred_element_type=f32)
        a = (_silu(g) * u).astype(bf16)
        part = jnp.dot(a, wd_ref[0, cs, :], preferred_element_type=f32)
        if f == 0:
            acc_ref[...] = part
        else:
            acc_ref[...] += part
    y = x + FFN_RES * acc_ref[...]
    if has_final:
        y = _rms(y, gf_ref[...])
    y_ref[...] = y


def _layer_resident(stack, layer):
    nd = stack.ndim
    return pl.BlockSpec((1,) + stack.shape[1:], lambda *_: (layer,) + (0,) * (nd - 1),
                        pipeline_mode=pl.Buffered(1))


def _ffn(x2, g, wg, wu, wd, layer, mix=None, final_g=None, tm=512):
    t, d = x2.shape
    f_dim = wg.shape[2]
    n_f = f_dim // FFN_F_TILE
    assert n_f * FFN_F_TILE == f_dim
    tm = _token_tile(t, tm)
    row = pl.BlockSpec((tm, d), lambda i: (i, 0))
    vec = _resident((1, d))
    args, specs = [x2], [row]
    if mix is not None:
        o, wo = mix
        args += [o, wo]
        specs += [pl.BlockSpec((tm, o.shape[1]), lambda i: (i, 0)), _resident(wo.shape)]
    args += [g.reshape(1, d), wg, wu, wd]
    specs += [vec, _layer_resident(wg, layer), _layer_resident(wu, layer), _layer_resident(wd, layer)]
    if final_g is not None:
        args.append(final_g.reshape(1, d))
        specs.append(vec)
    kern = functools.partial(_ffn_kernel, n_f=n_f, has_mix=mix is not None,
                             has_final=final_g is not None)
    return pl.pallas_call(
        kern, out_shape=jax.ShapeDtypeStruct((t, d), f32), grid=(t // tm,),
        in_specs=specs, out_specs=row,
        scratch_shapes=[pltpu.VMEM((tm, d), bf16), pltpu.VMEM((tm, d), f32)],
        compiler_params=_params(1), name="ffn")(*args)


def _conv_kernel(x_ref, past_ref, g_ref, w1_ref, b1_ref, wdw_ref, bdw_ref, lng_ref, lnb_ref,
                 w2_ref, b2_ref, y_ref, ns_ref, ubuf_ref, cbuf_ref, *, tm, d, kw):
    t = pl.program_id(1)
    n_past = kw - 1
    off = CONV_HALO - n_past

    @pl.when(t == 0)
    def _():
        ubuf_ref[0:CONV_HALO, :] = jnp.zeros((CONV_HALO, d), f32)
        ubuf_ref[off:CONV_HALO, :] = past_ref[0]

    x = x_ref[0]
    h = _rms(x, g_ref[...]).astype(bf16)
    a = jnp.dot(h, w1_ref[...], preferred_element_type=f32) + b1_ref[...]
    ubuf_ref[CONV_HALO:CONV_HALO + tm, :] = a[:, :d] * jax.nn.sigmoid(a[:, d:])

    rb = min(CONV_ROW_BLOCK, tm)
    for r0 in range(0, tm, rb):
        for c0 in range(0, d, CONV_LANE_BLOCK):
            cs = slice(c0, c0 + CONV_LANE_BLOCK)
            acc = jnp.broadcast_to(bdw_ref[:, cs], (rb, CONV_LANE_BLOCK))
            for rho in range(SUBLANES):
                taps = [k for k in range(kw) if (off + k) % SUBLANES == rho]
                if not taps:
                    continue
                span = rb + (off + taps[-1]) // SUBLANES * SUBLANES
                if rho == 0:
                    win = ubuf_ref[r0:r0 + span, cs]
                else:
                    n = span + SUBLANES
                    win = pltpu.roll(ubuf_ref[r0:r0 + n, cs], n - rho, axis=0)
                for k in taps:
                    a0 = (off + k) // SUBLANES * SUBLANES
                    acc = acc + wdw_ref[k:k + 1, cs] * win[a0:a0 + rb]
            cbuf_ref[r0:r0 + rb, cs] = acc

    c = cbuf_ref[...]
    xc = c - jnp.mean(c, axis=-1, keepdims=True)
    ln = xc * lax.rsqrt(jnp.mean(xc * xc, axis=-1, keepdims=True) + EPS) * lng_ref[...] + lnb_ref[...]
    m = jnp.dot(_silu(ln).astype(bf16), w2_ref[...], preferred_element_type=f32) + b2_ref[...]
    y_ref[0] = x + m

    ns_ref[0] = ubuf_ref[tm + off:tm + CONV_HALO, :]
    ubuf_ref[0:CONV_HALO, :] = ubuf_ref[tm:tm + CONV_HALO, :]


def _conv_mixer(x, past, g, w1, b1, wdw, bdw, lng, lnb, w2, b2, tm=256):
    b, l, d = x.shape
    kw = wdw.shape[0]
    assert kw - 1 <= CONV_HALO
    tm = _token_tile(l, tm)
    assert tm >= CONV_HALO and tm % 8 == 0
    tile = pl.BlockSpec((1, tm, d), lambda i, j: (i, j, 0))
    state = pl.BlockSpec((1, kw - 1, d), lambda i, j: (i, 0, 0))
    vec = _resident((1, d))
    kern = functools.partial(_conv_kernel, tm=tm, d=d, kw=kw)
    return pl.pallas_call(
        kern,
        out_shape=(jax.ShapeDtypeStruct((b, l, d), f32), jax.ShapeDtypeStruct((b, kw - 1, d), f32)),
        grid=(b, l // tm),
        in_specs=[tile, state, vec, _resident(w1.shape), _resident((1, 2 * d)), _resident(wdw.shape),
                  vec, vec, vec, _resident(w2.shape), vec],
        out_specs=(tile, state),
        scratch_shapes=[pltpu.VMEM((CONV_HALO + tm, d), f32), pltpu.VMEM((tm, d), f32)],
        compiler_params=_params(2), name="conv_mixer",
    )(x, past, g.reshape(1, d), w1, b1.reshape(1, 2 * d), wdw, bdw.reshape(1, d),
      lng.reshape(1, d), lnb.reshape(1, d), w2, b2.reshape(1, d))


def _qkv_kernel(*refs, d, k_major):
    if k_major:
        x_ref, g_ref, w_ref, wkt_ref, q_ref, k_ref, v_ref, kb_ref, vb_ref = refs
    else:
        x_ref, g_ref, w_ref, q_ref, k_ref, v_ref, kb_ref, vb_ref = refs
    h = _rms(x_ref[...], g_ref[...]).astype(bf16)
    q = jnp.dot(h, w_ref[:, 0:d], preferred_element_type=f32)
    q_ref[...] = (q * (HEAD_DIM ** -0.5)).astype(bf16)
    k = jnp.dot(h, w_ref[:, d:2 * d], preferred_element_type=f32)
    kb_ref[...] = k.astype(bf16)
    if k_major:
        k_ref[0] = lax.dot_general(wkt_ref[...], h, (((1,), (1,)), ((), ())),
                                   preferred_element_type=f32)
    else:
        k_ref[...] = k
    v = jnp.dot(h, w_ref[:, 2 * d:3 * d], preferred_element_type=f32)
    v_ref[...] = v
    vb_ref[...] = v.astype(bf16)


def _qkv(x2, g, w, wkt, seq_len, k_major, tm=512):
    t, d = x2.shape
    tm = _token_tile(seq_len if k_major else t, tm)
    nt = seq_len // tm
    row = pl.BlockSpec((tm, d), lambda i: (i, 0))
    args, specs = [x2, g.reshape(1, d), w], [row, _resident((1, d)), _resident(w.shape)]
    if k_major:
        args.append(wkt)
        specs.append(_resident(wkt.shape))
        k_shape = (t // seq_len, d, seq_len)
        k_spec = pl.BlockSpec((1, d, tm), lambda i: (i // nt, 0, i % nt))
    else:
        k_shape, k_spec = (t, d), row
    return pl.pallas_call(
        functools.partial(_qkv_kernel, d=d, k_major=k_major),
        out_shape=(jax.ShapeDtypeStruct((t, d), bf16), jax.ShapeDtypeStruct(k_shape, f32),
                   jax.ShapeDtypeStruct((t, d), f32), jax.ShapeDtypeStruct((t, d), bf16),
                   jax.ShapeDtypeStruct((t, d), bf16)),
        grid=(t // tm,),
        in_specs=specs, out_specs=(row, k_spec, row, row, row),
        compiler_params=_params(1), name="qkv")(*args)


def _chunk_of(pos):
    assert CHUNK & (CHUNK - 1) == 0
    return lax.shift_right_logical(pos, CHUNK.bit_length() - 1)


def _attn_kernel(*refs, hp, tq, tk, past_len, n_new, li):
    refs = list(refs)
    lam_ref, sub_ref, q_ref, k_ref, v_ref = refs[:5]
    refs = refs[5:]
    if past_len:
        pk_ref, pv_ref = refs[:2]
        refs = refs[2:]
    o_ref, qs_ref, m_ref, l_ref, acc_ref = refs
    qi = pl.program_id(2)
    q_pos0 = past_len + qi * tq
    heads = [slice(h * V_DIM, (h + 1) * V_DIM) for h in range(hp)]

    for h, hs in enumerate(heads):
        q = q_ref[0, :, hs]
        lane = lax.broadcasted_iota(jnp.int32, q.shape, 1)
        qs_ref[h, 0:tq, :] = jnp.where(lane < HEAD_DIM, q, jnp.zeros_like(q))
        qs_ref[h, tq:2 * tq, :] = jnp.where(lane >= HEAD_DIM, q, jnp.zeros_like(q))
    m_ref[...] = jnp.full_like(m_ref, NEG_BIG)
    l_ref[...] = jnp.zeros_like(l_ref)
    acc_ref[...] = jnp.zeros_like(acc_ref)

    def update(kv_of_head, n, k_pos0, k_on_rows=False):
        ones = jnp.ones((n, V_DIM), bf16)
        for h in range(hp):
            kt, vt = kv_of_head(h)
            s = lax.dot_general(qs_ref[h], kt, (((1,), (0 if k_on_rows else 1,)), ((), ())),
                                preferred_element_type=f32)
            if k_pos0 is not None:
                row = lax.broadcasted_iota(jnp.int32, s.shape, 0)
                qp = q_pos0 + jnp.where(row >= tq, row - tq, row)
                kp = k_pos0 + lax.broadcasted_iota(jnp.int32, s.shape, 1)
                s = jnp.where(_chunk_of(kp) <= _chunk_of(qp), s, NEG_BIG)
            m_prev = m_ref[h]
            m_new = jnp.maximum(m_prev, jnp.max(s, axis=-1, keepdims=True))
            alpha = jnp.exp(m_prev - m_new)
            if n % V_DIM == 0:
                m_wide = m_new if n == V_DIM else jnp.concatenate([m_new] * (n // V_DIM), axis=1)
            else:
                assert n < V_DIM
                m_wide = m_new[:, :n]
            p = jnp.exp(s - m_wide).astype(bf16)
            pv = jnp.dot(p, jnp.concatenate([vt, ones], axis=1), preferred_element_type=f32)
            acc_ref[h] = alpha * acc_ref[h] + pv[:, :V_DIM]
            l_ref[h] = alpha * l_ref[h] + pv[:, V_DIM:]
            m_ref[h] = m_new

    if past_len:
        update(lambda h: (pk_ref[0, heads[h], :].astype(bf16),
                          pv_ref[0, pl.ds(h, past_len, stride=hp), :].astype(bf16)),
               past_len, None, k_on_rows=True)

    first_end = (q_pos0 // CHUNK + 1) * CHUNK
    last_end = ((q_pos0 + tq - 1) // CHUNK + 1) * CHUNK
    n_full = jnp.clip((first_end - past_len) // tk, 0, n_new)
    n_need = jnp.clip((last_end - past_len + tk - 1) // tk, 0, n_new)

    def new_body(masked, j, carry):
        rows = pl.ds(pl.multiple_of(j * tk, tk), tk)
        update(lambda h: (k_ref[0, rows, heads[h]], v_ref[0, rows, heads[h]]), tk,
               past_len + j * tk if masked else None)
        return carry

    lax.fori_loop(0, n_full, functools.partial(new_body, False), 0)
    lax.fori_loop(n_full, n_need, functools.partial(new_body, True), 0)

    lam_rows = lam_ref[...]
    lam = (jnp.exp(jnp.sum(lam_rows[0:1] * lam_rows[1:2], axis=-1, keepdims=True))
           - jnp.exp(jnp.sum(lam_rows[2:3] * lam_rows[3:4], axis=-1, keepdims=True)) + li)
    for h, hs in enumerate(heads):
        o_all = acc_ref[h] / l_ref[h]
        o = o_all[0:tq] - lam * o_all[tq:2 * tq]
        o_ref[0, :, hs] = (_rms(o, sub_ref[...]) * (1.0 - li)).astype(bf16)


def _diff_attention(q, k, v, lam_rows, subln, li, past=None, past_row0=0, hp=2, tq=512, tk=512):
    b, l, d = q.shape
    n_heads = d // V_DIM
    assert n_heads % hp == 0
    w = hp * V_DIM
    tq = _token_tile(l, tq)
    tk = _token_tile(l, tk)
    past_len = 0 if past is None else past[0].shape[2]
    qtile = pl.BlockSpec((1, tq, w), lambda i, g, j: (i, j, g))
    seq = pl.BlockSpec((1, l, w), lambda i, g, j: (i, 0, g))
    args = [lam_rows, subln.reshape(1, V_DIM), q, k, v]
    specs = [_resident(lam_rows.shape), _resident((1, V_DIM)), qtile, seq, seq]
    if past_len:
        assert hp == n_heads and past[1].shape[1:] == (past_len * n_heads, V_DIM)
        args += list(past)
        specs += [pl.BlockSpec((1, d, past_len), lambda i, g, j: (past_row0 + i, 0, 0)),
                  pl.BlockSpec((1, past_len * n_heads, V_DIM), lambda i, g, j: (past_row0 + i, 0, 0))]
    kern = functools.partial(_attn_kernel, hp=hp, tq=tq, tk=tk, past_len=past_len,
                             n_new=l // tk, li=li)
    stat = pltpu.VMEM((hp, 2 * tq, V_DIM), f32)
    return pl.pallas_call(
        kern, out_shape=jax.ShapeDtypeStruct((b, l, d), bf16), grid=(b, n_heads // hp, l // tq),
        in_specs=specs, out_specs=qtile,
        scratch_shapes=[pltpu.VMEM((hp, 2 * tq, V_DIM), bf16), stat, stat, stat],
        compiler_params=_params(3), name="diff_attn")(*args)


def _lambda_init(layer):
    return 0.8 - 0.6 * math.exp(-0.3 * layer)


def _trunk(x, cache_k, cache_v, state_conv, w):
    b, l, d = x.shape
    depth = w['ffn1_norm'].shape[0]
    x2 = x.reshape(b * l, d)
    new_k, new_v, new_c = [], [], []
    for i in range(depth):
        j = i // N_MIXERS
        x2 = _ffn(x2, w['ffn1_norm'][i], *w['ffn1'], i)
        mix = None
        if i % N_MIXERS == 0:
            past = (jnp.zeros((b, w['conv_w_dw'].shape[1] - 1, d), f32) if state_conv is None
                    else state_conv[j])
            y, c_new = _conv_mixer(x2.reshape(b, l, d), past, w['mix_norm'][i], w['conv_w_pw1'][j],
                                   w['conv_b_pw1'][j], w['conv_w_dw'][j], w['conv_b_dw'][j],
                                   w['conv_ln_g'][j], w['conv_ln_b'][j], w['conv_w_pw2'][j],
                                   w['conv_b_pw2'][j])
            x2 = y.reshape(b * l, d)
            new_c.append(c_new)
        else:
            k_major = l % V_DIM == 0
            q, k, v, kb, vb = _qkv(x2, w['mix_norm'][i], w['attn_w_qkv'][j], w['attn_w_kt'][j], l, k_major)
            past, cfg = None, dict(hp=4)
            if cache_k is not None:
                n_attn, _, p_len = cache_k.shape[:3]
                past = (jnp.transpose(cache_k, (0, 1, 3, 4, 2)).reshape(n_attn * b, d, p_len),
                        cache_v.reshape(n_attn * b, p_len * N_HEADS, V_DIM))
                cfg = dict(hp=N_HEADS, past_row0=j * b)
            o = _diff_attention(q.reshape(b, l, d), kb.reshape(b, l, d), vb.reshape(b, l, d),
                                w['attn_lambda'][j], w['attn_subln'][j], _lambda_init(i), past, **cfg)
            mix = (o.reshape(b * l, d), w['attn_w_o'][j])
            if k_major:
                k = jnp.transpose(k.reshape(b, 2 * N_HEADS, HEAD_DIM, l), (0, 3, 1, 2))
            new_k.append(k.reshape(b, l, 2 * N_HEADS, HEAD_DIM))
            new_v.append(v.reshape(b, l, N_HEADS, V_DIM))
        x2 = _ffn(x2, w['ffn2_norm'][i], *w['ffn2'], i, mix=mix,
                  final_g=w['final_norm'] if i == depth - 1 else None)
    return x2.reshape(b, l, d), jnp.stack(new_k), jnp.stack(new_v), jnp.stack(new_c)


def kernel(x_prompt, x_sample, cache_k, cache_v, state_conv, ffn1_norm, ffn1_w_gate, ffn1_w_up, ffn1_w_down, mix_norm, ffn2_norm, ffn2_w_gate, ffn2_w_up, ffn2_w_down, conv_w_pw1, conv_b_pw1, conv_w_dw, conv_b_dw, conv_ln_g, conv_ln_b, conv_w_pw2, conv_b_pw2, attn_w_qkv, attn_lambda_q1, attn_lambda_k1, attn_lambda_q2, attn_lambda_k2, attn_subln, attn_w_o, final_norm):
    d = x_prompt.shape[-1]
    w = {
        'ffn1_norm': ffn1_norm, 'mix_norm': mix_norm, 'ffn2_norm': ffn2_norm, 'final_norm': final_norm,
        'ffn1': (ffn1_w_gate.astype(bf16), ffn1_w_up.astype(bf16), ffn1_w_down.astype(bf16)),
        'ffn2': (ffn2_w_gate.astype(bf16), ffn2_w_up.astype(bf16), ffn2_w_down.astype(bf16)),
        'conv_w_pw1': conv_w_pw1.astype(bf16), 'conv_b_pw1': conv_b_pw1, 'conv_w_dw': conv_w_dw,
        'conv_b_dw': conv_b_dw, 'conv_ln_g': conv_ln_g, 'conv_ln_b': conv_ln_b,
        'conv_w_pw2': conv_w_pw2.astype(bf16), 'conv_b_pw2': conv_b_pw2,
        'attn_w_qkv': attn_w_qkv.astype(bf16), 'attn_w_o': attn_w_o.astype(bf16),
        'attn_w_kt': jnp.transpose(attn_w_qkv[:, :, d:2 * d], (0, 2, 1)).astype(bf16),
        'attn_subln': attn_subln,
        'attn_lambda': jnp.stack([attn_lambda_q1, attn_lambda_k1, attn_lambda_q2, attn_lambda_k2], axis=1),
    }
    y_p, k_p, v_p, c_p = _trunk(x_prompt, None, None, None, w)
    y_s, k_s, v_s, c_s = _trunk(x_sample, cache_k, cache_v, state_conv, w)
    return (y_p, y_s, k_p, v_p, c_p, k_s, v_s, c_s)
```

```python
import functools
import math

import jax
import jax.numpy as jnp
from jax import lax
from jax.experimental import pallas as pl
from jax.experimental.pallas import tpu as pltpu

EPS = 1e-6
FFN_RES = 0.5
CHUNK = 64
N_HEADS = 8
HEAD_DIM = 64
V_DIM = 2 * HEAD_DIM
N_MIXERS = 2
NEG_BIG = -1e30

V7X_VMEM_BYTES = 64 * 1024 * 1024
VMEM_LIMIT_BYTES = V7X_VMEM_BYTES - 8 * 1024 * 1024
FFN_F_TILE = 256
CONV_HALO = 32
CONV_ROW_BLOCK = 64
CONV_LANE_BLOCK = 128
SUBLANES = 8

bf16 = jnp.bfloat16
f32 = jnp.float32


def _params(n_grid_axes):
    return pltpu.CompilerParams(
        dimension_semantics=("arbitrary",) * n_grid_axes,
        vmem_limit_bytes=VMEM_LIMIT_BYTES)


def _resident(shape):
    nd = len(shape)
    return pl.BlockSpec(shape, lambda *_: (0,) * nd, pipeline_mode=pl.Buffered(1))


def _rms(x, g):
    return x * lax.rsqrt(jnp.mean(x * x, axis=-1, keepdims=True) + EPS) * g


def _silu(x):
    return x * jax.nn.sigmoid(x)


def _token_tile(n_tokens, want):
    t = min(want, n_tokens)
    assert n_tokens % t == 0, (n_tokens, t)
    return t


def _ffn_kernel(*refs, n_f, has_mix, has_final):
    refs = list(refs)
    x_ref = refs.pop(0)
    if has_mix:
        o_ref_in, wo_ref = refs.pop(0), refs.pop(0)
    g_ref, wg_ref, wu_ref, wd_ref = refs[:4]
    refs = refs[4:]
    if has_final:
        gf_ref = refs.pop(0)
    y_ref, h_ref, acc_ref = refs

    x = x_ref[...]
    if has_mix:
        x = x + jnp.dot(o_ref_in[...], wo_ref[...], preferred_element_type=f32)
    h_ref[...] = _rms(x, g_ref[...]).astype(bf16)

    for f in range(n_f):
        cs = slice(f * FFN_F_TILE, (f + 1) * FFN_F_TILE)
        h = h_ref[...]
        g = jnp.dot(h, wg_ref[0, :, cs], preferred_element_type=f32)
        u = jnp.dot(h, wu_ref[0, :, cs], preferred_element_type=f32)
        a = (_silu(g) * u).astype(bf16)
        part = jnp.dot(a, wd_ref[0, cs, :], preferred_element_type=f32)
        if f == 0:
            acc_ref[...] = part
        else:
            acc_ref[...] += part
    y = x + FFN_RES * acc_ref[...]
    if has_final:
        y = _rms(y, gf_ref[...])
    y_ref[...] = y


def _ffn_stream_kernel(*refs, has_mix, has_final):
    refs = list(refs)
    x_ref = refs.pop(0)
    if has_mix:
        o_ref_in, wo_ref = refs.pop(0), refs.pop(0)
    g_ref, wg_ref, wu_ref, wd_ref = refs[:4]
    refs = refs[4:]
    if has_final:
        gf_ref = refs.pop(0)
    y_ref, xs_ref, h_ref, acc_ref = refs
    f = pl.program_id(0)

    @pl.when(f == 0)
    def _():
        x = x_ref[...]
        if has_mix:
            x = x + jnp.dot(o_ref_in[...], wo_ref[...], preferred_element_type=f32)
        xs_ref[...] = x
        h_ref[...] = _rms(x, g_ref[...]).astype(bf16)
        acc_ref[...] = jnp.zeros_like(acc_ref)

    h = h_ref[...]
    g = jnp.dot(h, wg_ref[0], preferred_element_type=f32)
    u = jnp.dot(h, wu_ref[0], preferred_element_type=f32)
    a = (_silu(g) * u).astype(bf16)
    acc_ref[...] += jnp.dot(a, wd_ref[0], preferred_element_type=f32)

    @pl.when(f == pl.num_programs(0) - 1)
    def _():
        y = xs_ref[...] + FFN_RES * acc_ref[...]
        if has_final:
            y = _rms(y, gf_ref[...])
        y_ref[...] = y


def _layer_resident(stack, layer):
    nd = stack.ndim
    return pl.BlockSpec((1,) + stack.shape[1:], lambda *_: (layer,) + (0,) * (nd - 1),
                        pipeline_mode=pl.Buffered(1))


def _ffn(x2, g, wg, wu, wd, layer, mix=None, final_g=None, tm=512):
    t, d = x2.shape
    f_dim = wg.shape[2]
    n_f = f_dim // FFN_F_TILE
    assert n_f * FFN_F_TILE == f_dim
    tm = _token_tile(t, tm)
    stream = t == tm
    row = pl.BlockSpec((tm, d), lambda i: (0, 0) if stream else (i, 0))
    vec = _resident((1, d))
    args, specs = [x2], [row]
    if mix is not None:
        o, wo = mix
        args += [o, wo]
        specs += [pl.BlockSpec((tm, o.shape[1]), lambda i: (0, 0) if stream else (i, 0)),
                  _resident(wo.shape)]
    args += [g.reshape(1, d), wg, wu, wd]
    if stream:
        col_chunk = pl.BlockSpec((1, d, FFN_F_TILE), lambda f: (layer, 0, f))
        specs += [vec, col_chunk, col_chunk, pl.BlockSpec((1, FFN_F_TILE, d), lambda f: (layer, f, 0))]
    else:
        specs += [vec, _layer_resident(wg, layer), _layer_resident(wu, layer), _layer_resident(wd, layer)]
    if final_g is not None:
        args.append(final_g.reshape(1, d))
        specs.append(vec)
    flags = dict(has_mix=mix is not None, has_final=final_g is not None)
    if stream:
        return pl.pallas_call(
            functools.partial(_ffn_stream_kernel, **flags),
            out_shape=jax.ShapeDtypeStruct((t, d), f32), grid=(n_f,), in_specs=specs, out_specs=row,
            scratch_shapes=[pltpu.VMEM((tm, d), f32), pltpu.VMEM((tm, d), bf16), pltpu.VMEM((tm, d), f32)],
            compiler_params=_params(1), name="ffn_stream")(*args)
    kern = functools.partial(_ffn_kernel, n_f=n_f, **flags)
    return pl.pallas_call(
        kern, out_shape=jax.ShapeDtypeStruct((t, d), f32), grid=(t // tm,),
        in_specs=specs, out_specs=row,
        scratch_shapes=[pltpu.VMEM((tm, d), bf16), pltpu.VMEM((tm, d), f32)],
        compiler_params=_params(1), name="ffn")(*args)


def _conv_kernel(x_ref, past_ref, g_ref, w1_ref, b1_ref, wdw_ref, bdw_ref, lng_ref, lnb_ref,
                 w2_ref, b2_ref, y_ref, ns_ref, ubuf_ref, cbuf_ref, *, tm, d, kw):
    t = pl.program_id(1)
    n_past = kw - 1
    off = CONV_HALO - n_past

    @pl.when(t == 0)
    def _():
        ubuf_ref[0:CONV_HALO, :] = jnp.zeros((CONV_HALO, d), f32)
        ubuf_ref[off:CONV_HALO, :] = past_ref[0]

    x = x_ref[0]
    h = _rms(x, g_ref[...]).astype(bf16)
    a = jnp.dot(h, w1_ref[...], preferred_element_type=f32) + b1_ref[...]
    ubuf_ref[CONV_HALO:CONV_HALO + tm, :] = a[:, :d] * jax.nn.sigmoid(a[:, d:])

    rb = min(CONV_ROW_BLOCK, tm)
    for r0 in range(0, tm, rb):
        for c0 in range(0, d, CONV_LANE_BLOCK):
            cs = slice(c0, c0 + CONV_LANE_BLOCK)
            acc = jnp.broadcast_to(bdw_ref[:, cs], (rb, CONV_LANE_BLOCK))
            for rho in range(SUBLANES):
                taps = [k for k in range(kw) if (off + k) % SUBLANES == rho]
                if not taps:
                    continue
                span = rb + (off + taps[-1]) // SUBLANES * SUBLANES
                if rho == 0:
                    win = ubuf_ref[r0:r0 + span, cs]
                else:
                    n = span + SUBLANES
                    win = pltpu.roll(ubuf_ref[r0:r0 + n, cs], n - rho, axis=0)
                for k in taps:
                    a0 = (off + k) // SUBLANES * SUBLANES
                    acc = acc + wdw_ref[k:k + 1, cs] * win[a0:a0 + rb]
            cbuf_ref[r0:r0 + rb, cs] = acc

    c = cbuf_ref[...]
    xc = c - jnp.mean(c, axis=-1, keepdims=True)
    ln = xc * lax.rsqrt(jnp.mean(xc * xc, axis=-1, keepdims=True) + EPS) * lng_ref[...] + lnb_ref[...]
    m = jnp.dot(_silu(ln).astype(bf16), w2_ref[...], preferred_element_type=f32) + b2_ref[...]
    y_ref[0] = x + m

    ns_ref[0] = ubuf_ref[tm + off:tm + CONV_HALO, :]
    ubuf_ref[0:CONV_HALO, :] = ubuf_ref[tm:tm + CONV_HALO, :]


def _conv_mixer(x, past, g, w1, b1, wdw, bdw, lng, lnb, w2, b2, tm=256):
    b, l, d = x.shape
    kw = wdw.shape[0]
    assert kw - 1 <= CONV_HALO
    tm = _token_tile(l, tm)
    assert tm >= CONV_HALO and tm % 8 == 0
    tile = pl.BlockSpec((1, tm, d), lambda i, j: (i, j, 0))
    state = pl.BlockSpec((1, kw - 1, d), lambda i, j: (i, 0, 0))
    vec = _resident((1, d))
    kern = functools.partial(_conv_kernel, tm=tm, d=d, kw=kw)
    return pl.pallas_call(
        kern,
        out_shape=(jax.ShapeDtypeStruct((b, l, d), f32), jax.ShapeDtypeStruct((b, kw - 1, d), f32)),
        grid=(b, l // tm),
        in_specs=[tile, state, vec, _resident(w1.shape), _resident((1, 2 * d)), _resident(wdw.shape),
                  vec, vec, vec, _resident(w2.shape), vec],
        out_specs=(tile, state),
        scratch_shapes=[pltpu.VMEM((CONV_HALO + tm, d), f32), pltpu.VMEM((tm, d), f32)],
        compiler_params=_params(2), name="conv_mixer",
    )(x, past, g.reshape(1, d), w1, b1.reshape(1, 2 * d), wdw, bdw.reshape(1, d),
      lng.reshape(1, d), lnb.reshape(1, d), w2, b2.reshape(1, d))


def _qkv_kernel(*refs, d, k_major):
    x_ref, g_ref, w_ref = refs[:3]
    q_ref, k_ref, v_ref, kb_ref, vb_ref = refs[-5:]
    h = _rms(x_ref[...], g_ref[...]).astype(bf16)
    q = jnp.dot(h, w_ref[:, 0:d], preferred_element_type=f32)
    q_ref[...] = (q * (HEAD_DIM ** -0.5)).astype(bf16)
    k = jnp.dot(h, w_ref[:, d:2 * d], preferred_element_type=f32)
    kb_ref[...] = k.astype(bf16)
    if k_major:
        k_ref[0, 0] = k.T
    else:
        k_ref[0] = k
    v = jnp.dot(h, w_ref[:, 2 * d:3 * d], preferred_element_type=f32)
    v_ref[0] = v
    vb_ref[...] = v.astype(bf16)


def _qkv(x2, g, w, seq_len, k_major, layer, n_layers, stacks=None, tm=512):
    t, d = x2.shape
    tm = _token_tile(seq_len if k_major else t, tm)
    nt = seq_len // tm
    nb = t // seq_len
    row = pl.BlockSpec((tm, d), lambda i: (i, 0))
    args, specs = [x2, g.reshape(1, d), w], [row, _resident((1, d)), _resident(w.shape)]
    aliases = {}
    if stacks is not None:
        aliases = {len(args): 1, len(args) + 1: 2}
        args += list(stacks)
        specs += [pl.BlockSpec(memory_space=pl.ANY)] * 2
    if k_major:
        k_shape = (n_layers, nb, d, seq_len)
        k_spec = pl.BlockSpec((1, 1, d, tm), lambda i: (layer, i // nt, 0, i % nt))
    else:
        k_shape = (n_layers, t, d)
        k_spec = pl.BlockSpec((1, tm, d), lambda i: (layer, i, 0))
    v_spec = pl.BlockSpec((1, tm, d), lambda i: (layer, i, 0))
    return pl.pallas_call(
        functools.partial(_qkv_kernel, d=d, k_major=k_major),
        out_shape=(jax.ShapeDtypeStruct((t, d), bf16), jax.ShapeDtypeStruct(k_shape, f32),
                   jax.ShapeDtypeStruct((n_layers, t, d), f32), jax.ShapeDtypeStruct((t, d), bf16),
                   jax.ShapeDtypeStruct((t, d), bf16)),
        grid=(t // tm,),
        in_specs=specs, out_specs=(row, k_spec, v_spec, row, row),
        input_output_aliases=aliases,
        compiler_params=_params(1), name="qkv")(*args)


def _chunk_of(pos):
    assert CHUNK & (CHUNK - 1) == 0
    return lax.shift_right_logical(pos, CHUNK.bit_length() - 1)


def _attn_kernel(*refs, hp, tq, tk, past_len, n_new, li):
    refs = list(refs)
    lam_ref, sub_ref, q_ref, k_ref, v_ref = refs[:5]
    refs = refs[5:]
    if past_len:
        pk_ref, pv_ref = refs[:2]
        refs = refs[2:]
    o_ref, qs_ref, m_ref, l_ref, acc_ref = refs
    qi = pl.program_id(2)
    q_pos0 = past_len + qi * tq
    heads = [slice(h * V_DIM, (h + 1) * V_DIM) for h in range(hp)]

    for h, hs in enumerate(heads):
        q = q_ref[0, :, hs]
        lane = lax.broadcasted_iota(jnp.int32, q.shape, 1)
        qs_ref[h, 0:tq, :] = jnp.where(lane < HEAD_DIM, q, jnp.zeros_like(q))
        qs_ref[h, tq:2 * tq, :] = jnp.where(lane >= HEAD_DIM, q, jnp.zeros_like(q))
    m_ref[...] = jnp.full_like(m_ref, NEG_BIG)
    l_ref[...] = jnp.zeros_like(l_ref)
    acc_ref[...] = jnp.zeros_like(acc_ref)

    def update(kv_of_head, n, k_pos0, k_on_rows=False):
        ones = jnp.ones((n, V_DIM), bf16)
        for h in range(hp):
            kt, vt = kv_of_head(h)
            s = lax.dot_general(qs_ref[h], kt, (((1,), (0 if k_on_rows else 1,)), ((), ())),
                                preferred_element_type=f32)
            if k_pos0 is not None:
                row = lax.broadcasted_iota(jnp.int32, s.shape, 0)
                qp = q_pos0 + jnp.where(row >= tq, row - tq, row)
                kp = k_pos0 + lax.broadcasted_iota(jnp.int32, s.shape, 1)
                s = jnp.where(_chunk_of(kp) <= _chunk_of(qp), s, NEG_BIG)
            m_prev = m_ref[h]
            m_new = jnp.maximum(m_prev, jnp.max(s, axis=-1, keepdims=True))
            alpha = jnp.exp(m_prev - m_new)
            if n % V_DIM == 0:
                m_wide = m_new if n == V_DIM else jnp.concatenate([m_new] * (n // V_DIM), axis=1)
            else:
                assert n < V_DIM
                m_wide = m_new[:, :n]
            p = jnp.exp(s - m_wide).astype(bf16)
            pv = jnp.dot(p, jnp.concatenate([vt, ones], axis=1), preferred_element_type=f32)
            acc_ref[h] = alpha * acc_ref[h] + pv[:, :V_DIM]
            l_ref[h] = alpha * l_ref[h] + pv[:, V_DIM:]
            m_ref[h] = m_new

    if past_len:
        update(lambda h: (pk_ref[0, heads[h], :].astype(bf16),
                          pv_ref[0, pl.ds(h, past_len, stride=hp), :].astype(bf16)),
               past_len, None, k_on_rows=True)

    first_end = (q_pos0 // CHUNK + 1) * CHUNK
    last_end = ((q_pos0 + tq - 1) // CHUNK + 1) * CHUNK
    n_full = jnp.clip((first_end - past_len) // tk, 0, n_new)
    n_need = jnp.clip((last_end - past_len + tk - 1) // tk, 0, n_new)

    def new_body(masked, j, carry):
        rows = pl.ds(pl.multiple_of(j * tk, tk), tk)
        update(lambda h: (k_ref[0, rows, heads[h]], v_ref[0, rows, heads[h]]), tk,
               past_len + j * tk if masked else None)
        return carry

    def pair_body(jj, carry):
        new_body(False, 2 * jj, carry)
        return new_body(False, 2 * jj + 1, carry)

    n_pairs = n_full // 2
    lax.fori_loop(0, n_pairs, pair_body, 0)
    lax.fori_loop(2 * n_pairs, n_full, functools.partial(new_body, False), 0)
    lax.fori_loop(n_full, n_need, functools.partial(new_body, True), 0)

    lam_rows = lam_ref[...]
    lam = (jnp.exp(jnp.sum(lam_rows[0:1] * lam_rows[1:2], axis=-1, keepdims=True))
           - jnp.exp(jnp.sum(lam_rows[2:3] * lam_rows[3:4], axis=-1, keepdims=True)) + li)
    for h, hs in enumerate(heads):
        o_all = acc_ref[h] / l_ref[h]
        o = o_all[0:tq] - lam * o_all[tq:2 * tq]
        o_ref[0, :, hs] = (_rms(o, sub_ref[...]) * (1.0 - li)).astype(bf16)


def _diff_attention(q, k, v, lam_rows, subln, li, past=None, past_row0=0, hp=2, tq=512, tk=512):
    b, l, d = q.shape
    n_heads = d // V_DIM
    assert n_heads % hp == 0
    w = hp * V_DIM
    tq = _token_tile(l, tq)
    tk = _token_tile(l, tk)
    past_len = 0 if past is None else past[0].shape[2]
    qtile = pl.BlockSpec((1, tq, w), lambda i, g, j: (i, j, g))
    seq = pl.BlockSpec((1, l, w), lambda i, g, j: (i, 0, g))
    args = [lam_rows, subln.reshape(1, V_DIM), q, k, v]
    specs = [_resident(lam_rows.shape), _resident((1, V_DIM)), qtile, seq, seq]
    if past_len:
        assert hp == n_heads and past[1].shape[1:] == (past_len * n_heads, V_DIM)
        args += list(past)
        specs += [pl.BlockSpec((1, d, past_len), lambda i, g, j: (past_row0 + i, 0, 0)),
                  pl.BlockSpec((1, past_len * n_heads, V_DIM), lambda i, g, j: (past_row0 + i, 0, 0))]
    kern = functools.partial(_attn_kernel, hp=hp, tq=tq, tk=tk, past_len=past_len,
                             n_new=l // tk, li=li)
    stat = pltpu.VMEM((hp, 2 * tq, V_DIM), f32)
    return pl.pallas_call(
        kern, out_shape=jax.ShapeDtypeStruct((b, l, d), bf16), grid=(b, n_heads // hp, l // tq),
        in_specs=specs, out_specs=qtile,
        scratch_shapes=[pltpu.VMEM((hp, 2 * tq, V_DIM), bf16), stat, stat, stat],
        compiler_params=_params(3), name="diff_attn")(*args)


def _lambda_init(layer):
    return 0.8 - 0.6 * math.exp(-0.3 * layer)


def _trunk(x, cache_k, cache_v, state_conv, w):
    b, l, d = x.shape
    depth = w['ffn1_norm'].shape[0]
    n_attn = len([i for i in range(depth) if i % N_MIXERS == 1])
    k_major = l % V_DIM == 0
    x2 = x.reshape(b * l, d)
    kv_stacks, new_c = None, []
    for i in range(depth):
        j = i // N_MIXERS
        x2 = _ffn(x2, w['ffn1_norm'][i], *w['ffn1'], i)
        mix = None
        if i % N_MIXERS == 0:
            past = (jnp.zeros((b, w['conv_w_dw'].shape[1] - 1, d), f32) if state_conv is None
                    else state_conv[j])
            y, c_new = _conv_mixer(x2.reshape(b, l, d), past, w['mix_norm'][i], w['conv_w_pw1'][j],
                                   w['conv_b_pw1'][j], w['conv_w_dw'][j], w['conv_b_dw'][j],
                                   w['conv_ln_g'][j], w['conv_ln_b'][j], w['conv_w_pw2'][j],
                                   w['conv_b_pw2'][j])
            x2 = y.reshape(b * l, d)
            new_c.append(c_new)
        else:
            q, k_all, v_all, kb, vb = _qkv(x2, w['mix_norm'][i], w['attn_w_qkv'][j], l, k_major,
                                           j, n_attn, kv_stacks)
            kv_stacks = (k_all, v_all)
            past, cfg = None, dict(hp=4)
            if cache_k is not None:
                p_len = cache_k.shape[2]
                past = (jnp.transpose(cache_k, (0, 1, 3, 4, 2)).reshape(n_attn * b, d, p_len),
                        cache_v.reshape(n_attn * b, p_len * N_HEADS, V_DIM))
                cfg = dict(hp=N_HEADS, past_row0=j * b)
            o = _diff_attention(q.reshape(b, l, d), kb.reshape(b, l, d), vb.reshape(b, l, d),
                                w['attn_lambda'][j], w['attn_subln'][j], _lambda_init(i), past, **cfg)
            mix = (o.reshape(b * l, d), w['attn_w_o'][j])
        x2 = _ffn(x2, w['ffn2_norm'][i], *w['ffn2'], i, mix=mix,
                  final_g=w['final_norm'] if i == depth - 1 else None)
    k_all, v_all = kv_stacks
    if k_major:
        k_all = jnp.transpose(k_all.reshape(n_attn, b, 2 * N_HEADS, HEAD_DIM, l), (0, 1, 4, 2, 3))
    return (x2.reshape(b, l, d), k_all.reshape(n_attn, b, l, 2 * N_HEADS, HEAD_DIM),
            v_all.reshape(n_attn, b, l, N_HEADS, V_DIM), jnp.stack(new_c))


def kernel(x_prompt, x_sample, cache_k, cache_v, state_conv, ffn1_norm, ffn1_w_gate, ffn1_w_up, ffn1_w_down, mix_norm, ffn2_norm, ffn2_w_gate, ffn2_w_up, ffn2_w_down, conv_w_pw1, conv_b_pw1, conv_w_dw, conv_b_dw, conv_ln_g, conv_ln_b, conv_w_pw2, conv_b_pw2, attn_w_qkv, attn_lambda_q1, attn_lambda_k1, attn_lambda_q2, attn_lambda_k2, attn_subln, attn_w_o, final_norm):
    w = {
        'ffn1_norm': ffn1_norm, 'mix_norm': mix_norm, 'ffn2_norm': ffn2_norm, 'final_norm': final_norm,
        'ffn1': (ffn1_w_gate.astype(bf16), ffn1_w_up.astype(bf16), ffn1_w_down.astype(bf16)),
        'ffn2': (ffn2_w_gate.astype(bf16), ffn2_w_up.astype(bf16), ffn2_w_down.astype(bf16)),
        'conv_w_pw1': conv_w_pw1.astype(bf16), 'conv_b_pw1': conv_b_pw1, 'conv_w_dw': conv_w_dw,
        'conv_b_dw': conv_b_dw, 'conv_ln_g': conv_ln_g, 'conv_ln_b': conv_ln_b,
        'conv_w_pw2': conv_w_pw2.astype(bf16), 'conv_b_pw2': conv_b_pw2,
        'attn_w_qkv': attn_w_qkv.astype(bf16), 'attn_w_o': attn_w_o.astype(bf16),
        'attn_subln': attn_subln,
        'attn_lambda': jnp.stack([attn_lambda_q1, attn_lambda_k1, attn_lambda_q2, attn_lambda_k2], axis=1),
    }
    y_p, k_p, v_p, c_p = _trunk(x_prompt, None, None, None, w)
    y_s, k_s, v_s, c_s = _trunk(x_sample, cache_k, cache_v, state_conv, w)
    return (y_p, y_s, k_p, v_p, c_p, k_s, v_s, c_s)
```

```python
import functools
import math

import jax
import jax.numpy as jnp
from jax import lax
from jax.experimental import pallas as pl
from jax.experimental.pallas import tpu as pltpu

EPS = 1e-6
FFN_RES = 0.5
CHUNK = 64
N_HEADS = 8
HEAD_DIM = 64
V_DIM = 2 * HEAD_DIM
N_MIXERS = 2
NEG_BIG = -1e30

V7X_VMEM_BYTES = 64 * 1024 * 1024
VMEM_LIMIT_BYTES = V7X_VMEM_BYTES - 8 * 1024 * 1024
FFN_F_TILE = 256
CONV_HALO = 32
CONV_ROW_BLOCK = 64
CONV_LANE_BLOCK = 128
SUBLANES = 8

bf16 = jnp.bfloat16
f32 = jnp.float32


def _params(n_grid_axes):
    return pltpu.CompilerParams(
        dimension_semantics=("arbitrary",) * n_grid_axes,
        vmem_limit_bytes=VMEM_LIMIT_BYTES)


def _resident(shape):
    nd = len(shape)
    return pl.BlockSpec(shape, lambda *_: (0,) * nd, pipeline_mode=pl.Buffered(1))


def _rms(x, g):
    return x * lax.rsqrt(jnp.mean(x * x, axis=-1, keepdims=True) + EPS) * g


def _silu(x):
    return x * jax.nn.sigmoid(x)


def _token_tile(n_tokens, want):
    t = min(want, n_tokens)
    assert n_tokens % t == 0, (n_tokens, t)
    return t


def _ffn_chunks(h_ref, wg_ref, wu_ref, wd_ref, acc_ref, n_f):
    for f in range(n_f):
        cs = slice(f * FFN_F_TILE, (f + 1) * FFN_F_TILE)
        h = h_ref[...]
        g = jnp.dot(h, wg_ref[0, :, cs], preferred_element_type=f32)
        u = jnp.dot(h, wu_ref[0, :, cs], preferred_element_type=f32)
        a = (_silu(g) * u).astype(bf16)
        part = jnp.dot(a, wd_ref[0, cs, :], preferred_element_type=f32)
        if f == 0:
            acc_ref[...] = part
        else:
            acc_ref[...] += part


def _ffn_kernel(*refs, n_f, has_mix, has_final):
    refs = list(refs)
    x_ref = refs.pop(0)
    if has_mix:
        o_ref_in, wo_ref = refs.pop(0), refs.pop(0)
    g_ref, wg_ref, wu_ref, wd_ref = refs[:4]
    refs = refs[4:]
    if has_final:
        gf_ref = refs.pop(0)
    y_ref, h_ref, acc_ref = refs

    x = x_ref[...]
    if has_mix:
        x = x + jnp.dot(o_ref_in[...], wo_ref[...], preferred_element_type=f32)
    h_ref[...] = _rms(x, g_ref[...]).astype(bf16)
    _ffn_chunks(h_ref, wg_ref, wu_ref, wd_ref, acc_ref, n_f)
    y = x + FFN_RES * acc_ref[...]
    if has_final:
        y = _rms(y, gf_ref[...])
    y_ref[...] = y


def _ffn_stream_kernel(*refs, has_mix, has_final):
    refs = list(refs)
    x_ref = refs.pop(0)
    if has_mix:
        o_ref_in, wo_ref = refs.pop(0), refs.pop(0)
    g_ref, wg_ref, wu_ref, wd_ref = refs[:4]
    refs = refs[4:]
    if has_final:
        gf_ref = refs.pop(0)
    y_ref, xs_ref, h_ref, acc_ref = refs
    f = pl.program_id(0)

    @pl.when(f == 0)
    def _():
        x = x_ref[...]
        if has_mix:
            x = x + jnp.dot(o_ref_in[...], wo_ref[...], preferred_element_type=f32)
        xs_ref[...] = x
        h_ref[...] = _rms(x, g_ref[...]).astype(bf16)
        acc_ref[...] = jnp.zeros_like(acc_ref)

    h = h_ref[...]
    g = jnp.dot(h, wg_ref[0], preferred_element_type=f32)
    u = jnp.dot(h, wu_ref[0], preferred_element_type=f32)
    a = (_silu(g) * u).astype(bf16)
    acc_ref[...] += jnp.dot(a, wd_ref[0], preferred_element_type=f32)

    @pl.when(f == pl.num_programs(0) - 1)
    def _():
        y = xs_ref[...] + FFN_RES * acc_ref[...]
        if has_final:
            y = _rms(y, gf_ref[...])
        y_ref[...] = y


def _layer_resident(stack, layer):
    nd = stack.ndim
    return pl.BlockSpec((1,) + stack.shape[1:], lambda *_: (layer,) + (0,) * (nd - 1),
                        pipeline_mode=pl.Buffered(1))


def _ffn(x2, g, wg, wu, wd, layer, mix=None, final_g=None, tm=512):
    t, d = x2.shape
    f_dim = wg.shape[2]
    n_f = f_dim // FFN_F_TILE
    assert n_f * FFN_F_TILE == f_dim
    tm = _token_tile(t, tm)
    stream = t == tm
    row = pl.BlockSpec((tm, d), lambda i: (0, 0) if stream else (i, 0))
    vec = _resident((1, d))
    args, specs = [x2], [row]
    if mix is not None:
        o, wo = mix
        args += [o, wo]
        specs += [pl.BlockSpec((tm, o.shape[1]), lambda i: (0, 0) if stream else (i, 0)),
                  _resident(wo.shape)]
    args += [g.reshape(1, d), wg, wu, wd]
    if stream:
        col_chunk = pl.BlockSpec((1, d, FFN_F_TILE), lambda f: (layer, 0, f))
        specs += [vec, col_chunk, col_chunk, pl.BlockSpec((1, FFN_F_TILE, d), lambda f: (layer, f, 0))]
    else:
        specs += [vec, _layer_resident(wg, layer), _layer_resident(wu, layer), _layer_resident(wd, layer)]
    if final_g is not None:
        args.append(final_g.reshape(1, d))
        specs.append(vec)
    flags = dict(has_mix=mix is not None, has_final=final_g is not None)
    if stream:
        return pl.pallas_call(
            functools.partial(_ffn_stream_kernel, **flags),
            out_shape=jax.ShapeDtypeStruct((t, d), f32), grid=(n_f,), in_specs=specs, out_specs=row,
            scratch_shapes=[pltpu.VMEM((tm, d), f32), pltpu.VMEM((tm, d), bf16), pltpu.VMEM((tm, d), f32)],
            compiler_params=_params(1), name="ffn_stream")(*args)
    kern = functools.partial(_ffn_kernel, n_f=n_f, **flags)
    return pl.pallas_call(
        kern, out_shape=jax.ShapeDtypeStruct((t, d), f32), grid=(t // tm,),
        in_specs=specs, out_specs=row,
        scratch_shapes=[pltpu.VMEM((tm, d), bf16), pltpu.VMEM((tm, d), f32)],
        compiler_params=_params(1), name="ffn")(*args)


def _conv_halo_init(past_ref, ubuf_ref, d, kw):
    ubuf_ref[0:CONV_HALO, :] = jnp.zeros((CONV_HALO, d), f32)
    ubuf_ref[CONV_HALO - (kw - 1):CONV_HALO, :] = past_ref[0]


def _conv_halo_carry(ns_ref, ubuf_ref, tm, kw):
    ns_ref[0] = ubuf_ref[tm + CONV_HALO - (kw - 1):tm + CONV_HALO, :]
    ubuf_ref[0:CONV_HALO, :] = ubuf_ref[tm:tm + CONV_HALO, :]


def _conv_front(x, w_refs, ubuf_ref, tm, d):
    g_ref, w1_ref, b1_ref = w_refs[:3]
    h = _rms(x, g_ref[...]).astype(bf16)
    a = jnp.dot(h, w1_ref[...], preferred_element_type=f32) + b1_ref[...]
    ubuf_ref[CONV_HALO:CONV_HALO + tm, :] = a[:, :d] * jax.nn.sigmoid(a[:, d:])


def _conv_block_fns(w_refs, ubuf_ref, cbuf_ref, tm, d, kw):
    wdw_ref, bdw_ref = w_refs[3:5]
    off = CONV_HALO - (kw - 1)
    rb = min(CONV_ROW_BLOCK, tm)

    def block(r0, c0):
        cs = slice(c0, c0 + CONV_LANE_BLOCK)
        acc = jnp.broadcast_to(bdw_ref[:, cs], (rb, CONV_LANE_BLOCK))
        for rho in range(SUBLANES):
            taps = [k for k in range(kw) if (off + k) % SUBLANES == rho]
            if not taps:
                continue
            span = rb + (off + taps[-1]) // SUBLANES * SUBLANES
            if rho == 0:
                win = ubuf_ref[r0:r0 + span, cs]
            else:
                n = span + SUBLANES
                win = pltpu.roll(ubuf_ref[r0:r0 + n, cs], n - rho, axis=0)
            for k in taps:
                a0 = (off + k) // SUBLANES * SUBLANES
                acc = acc + wdw_ref[k:k + 1, cs] * win[a0:a0 + rb]
        cbuf_ref[r0:r0 + rb, cs] = acc

    return [functools.partial(block, r0, c0)
            for r0 in range(0, tm, rb) for c0 in range(0, d, CONV_LANE_BLOCK)]


def _conv_back(x, w_refs, cbuf_ref):
    lng_ref, lnb_ref, w2_ref, b2_ref = w_refs[5:]
    c = cbuf_ref[...]
    xc = c - jnp.mean(c, axis=-1, keepdims=True)
    ln = xc * lax.rsqrt(jnp.mean(xc * xc, axis=-1, keepdims=True) + EPS) * lng_ref[...] + lnb_ref[...]
    m = jnp.dot(_silu(ln).astype(bf16), w2_ref[...], preferred_element_type=f32) + b2_ref[...]
    return x + m


def _conv_tile(x, w_refs, ubuf_ref, cbuf_ref, tm, d, kw):
    _conv_front(x, w_refs, ubuf_ref, tm, d)
    for fn in _conv_block_fns(w_refs, ubuf_ref, cbuf_ref, tm, d, kw):
        fn()
    return _conv_back(x, w_refs, cbuf_ref)


def _conv_kernel(x_ref, past_ref, *refs, tm, d, kw):
    w_refs = refs[:9]
    y_ref, ns_ref, ubuf_ref, cbuf_ref = refs[9:]

    @pl.when(pl.program_id(1) == 0)
    def _():
        _conv_halo_init(past_ref, ubuf_ref, d, kw)

    y_ref[0] = _conv_tile(x_ref[0], w_refs, ubuf_ref, cbuf_ref, tm, d, kw)
    _conv_halo_carry(ns_ref, ubuf_ref, tm, kw)


def _conv_mixer(x, past, g, w1, b1, wdw, bdw, lng, lnb, w2, b2, tm=512):
    b, l, d = x.shape
    kw = wdw.shape[0]
    assert kw - 1 <= CONV_HALO
    tm = _token_tile(l, tm)
    assert tm >= CONV_HALO and tm % 8 == 0
    tile = pl.BlockSpec((1, tm, d), lambda i, j: (i, j, 0))
    state = pl.BlockSpec((1, kw - 1, d), lambda i, j: (i, 0, 0))
    vec = _resident((1, d))
    kern = functools.partial(_conv_kernel, tm=tm, d=d, kw=kw)
    return pl.pallas_call(
        kern,
        out_shape=(jax.ShapeDtypeStruct((b, l, d), f32), jax.ShapeDtypeStruct((b, kw - 1, d), f32)),
        grid=(b, l // tm),
        in_specs=[tile, state, vec, _resident(w1.shape), _resident((1, 2 * d)), _resident(wdw.shape),
                  vec, vec, vec, _resident(w2.shape), vec],
        out_specs=(tile, state),
        scratch_shapes=[pltpu.VMEM((CONV_HALO + tm, d), f32), pltpu.VMEM((tm, d), f32)],
        compiler_params=_params(2), name="conv_mixer",
    )(x, past, g.reshape(1, d), w1, b1.reshape(1, 2 * d), wdw, bdw.reshape(1, d),
      lng.reshape(1, d), lnb.reshape(1, d), w2, b2.reshape(1, d))


def _qkv_kernel(*refs, d, k_major):
    x_ref, g_ref, w_ref = refs[:3]
    q_ref, k_ref, v_ref, kb_ref, vb_ref = refs[-5:]
    h = _rms(x_ref[...], g_ref[...]).astype(bf16)
    q = jnp.dot(h, w_ref[:, 0:d], preferred_element_type=f32)
    q_ref[...] = (q * (HEAD_DIM ** -0.5)).astype(bf16)
    k = jnp.dot(h, w_ref[:, d:2 * d], preferred_element_type=f32)
    kb_ref[...] = k.astype(bf16)
    if k_major:
        k_ref[0, 0] = k.T
    else:
        k_ref[0] = k
    v = jnp.dot(h, w_ref[:, 2 * d:3 * d], preferred_element_type=f32)
    v_ref[0] = v
    vb_ref[...] = v.astype(bf16)


def _qkv(x2, g, w, seq_len, k_major, layer, n_layers, stacks=None, tm=512):
    t, d = x2.shape
    tm = _token_tile(seq_len if k_major else t, tm)
    nt = seq_len // tm
    nb = t // seq_len
    row = pl.BlockSpec((tm, d), lambda i: (i, 0))
    args, specs = [x2, g.reshape(1, d), w], [row, _resident((1, d)), _resident(w.shape)]
    aliases = {}
    if stacks is not None:
        aliases = {len(args): 1, len(args) + 1: 2}
        args += list(stacks)
        specs += [pl.BlockSpec(memory_space=pl.ANY)] * 2
    if k_major:
        k_shape = (n_layers, nb, d, seq_len)
        k_spec = pl.BlockSpec((1, 1, d, tm), lambda i: (layer, i // nt, 0, i % nt))
    else:
        k_shape = (n_layers, t, d)
        k_spec = pl.BlockSpec((1, tm, d), lambda i: (layer, i, 0))
    v_spec = pl.BlockSpec((1, tm, d), lambda i: (layer, i, 0))
    return pl.pallas_call(
        functools.partial(_qkv_kernel, d=d, k_major=k_major),
        out_shape=(jax.ShapeDtypeStruct((t, d), bf16), jax.ShapeDtypeStruct(k_shape, f32),
                   jax.ShapeDtypeStruct((n_layers, t, d), f32), jax.ShapeDtypeStruct((t, d), bf16),
                   jax.ShapeDtypeStruct((t, d), bf16)),
        grid=(t // tm,),
        in_specs=specs, out_specs=(row, k_spec, v_spec, row, row),
        input_output_aliases=aliases,
        compiler_params=_params(1), name="qkv")(*args)


def _chunk_of(pos):
    assert CHUNK & (CHUNK - 1) == 0
    return lax.shift_right_logical(pos, CHUNK.bit_length() - 1)


def _attn_kernel(*refs, hp, tq, tk, past_len, n_new, li):
    refs = list(refs)
    lam_ref, sub_ref, q_ref, k_ref, v_ref = refs[:5]
    refs = refs[5:]
    if past_len:
        pk_ref, pv_ref = refs[:2]
        refs = refs[2:]
    o_ref, qs_ref, m_ref, l_ref, acc_ref = refs
    qi = pl.program_id(2)
    q_pos0 = past_len + qi * tq
    heads = [slice(h * V_DIM, (h + 1) * V_DIM) for h in range(hp)]

    for h, hs in enumerate(heads):
        q = q_ref[0, :, hs]
        lane = lax.broadcasted_iota(jnp.int32, q.shape, 1)
        qs_ref[h, 0:tq, :] = jnp.where(lane < HEAD_DIM, q, jnp.zeros_like(q))
        qs_ref[h, tq:2 * tq, :] = jnp.where(lane >= HEAD_DIM, q, jnp.zeros_like(q))
    m_ref[...] = jnp.full_like(m_ref, NEG_BIG)
    l_ref[...] = jnp.zeros_like(l_ref)
    acc_ref[...] = jnp.zeros_like(acc_ref)

    def scores(h, kt, k_pos0, k_on_rows):
        s = lax.dot_general(qs_ref[h], kt, (((1,), (0 if k_on_rows else 1,)), ((), ())),
                            preferred_element_type=f32)
        if k_pos0 is not None:
            row = lax.broadcasted_iota(jnp.int32, s.shape, 0)
            qp = q_pos0 + jnp.where(row >= tq, row - tq, row)
            kp = k_pos0 + lax.broadcasted_iota(jnp.int32, s.shape, 1)
            s = jnp.where(_chunk_of(kp) <= _chunk_of(qp), s, NEG_BIG)
        return s

    def absorb(h, s, vt):
        n = s.shape[1]
        m_prev = m_ref[h]
        m_new = jnp.maximum(m_prev, jnp.max(s, axis=-1, keepdims=True))
        alpha = jnp.exp(m_prev - m_new)
        if n % V_DIM == 0:
            m_wide = m_new if n == V_DIM else jnp.concatenate([m_new] * (n // V_DIM), axis=1)
        else:
            assert n < V_DIM
            m_wide = m_new[:, :n]
        p = jnp.exp(s - m_wide).astype(bf16)
        ones = jnp.ones((n, V_DIM), bf16)
        pv = jnp.dot(p, jnp.concatenate([vt, ones], axis=1), preferred_element_type=f32)
        acc_ref[h] = alpha * acc_ref[h] + pv[:, :V_DIM]
        l_ref[h] = alpha * l_ref[h] + pv[:, V_DIM:]
        m_ref[h] = m_new

    def update(units, k_on_rows=False):
        score_of = lambda u: scores(u[0], u[1](), u[3], k_on_rows)
        nxt = score_of(units[0])
        for i, u in enumerate(units):
            cur, nxt = nxt, (score_of(units[i + 1]) if i + 1 < len(units) else None)
            absorb(u[0], cur, u[2]())

    if past_len:
        update([(h, functools.partial(lambda h: pk_ref[0, heads[h], :].astype(bf16), h),
                 functools.partial(lambda h: pv_ref[0, pl.ds(h, past_len, stride=hp), :].astype(bf16), h),
                 None) for h in range(hp)], k_on_rows=True)

    first_end = (q_pos0 // CHUNK + 1) * CHUNK
    last_end = ((q_pos0 + tq - 1) // CHUNK + 1) * CHUNK
    n_full = jnp.clip((first_end - past_len) // tk, 0, n_new)
    n_need = jnp.clip((last_end - past_len + tk - 1) // tk, 0, n_new)

    def tile_units(j, masked):
        rows = pl.ds(pl.multiple_of(j * tk, tk), tk)
        return [(h, functools.partial(lambda h: k_ref[0, rows, heads[h]], h),
                 functools.partial(lambda h: v_ref[0, rows, heads[h]], h),
                 past_len + j * tk if masked else None) for h in range(hp)]

    def new_body(masked, j, carry):
        update(tile_units(j, masked))
        return carry

    def pair_body(jj, carry):
        update(tile_units(2 * jj, False) + tile_units(2 * jj + 1, False))
        return carry

    n_pairs = n_full // 2
    lax.fori_loop(0, n_pairs, pair_body, 0)
    lax.fori_loop(2 * n_pairs, n_full, functools.partial(new_body, False), 0)
    lax.fori_loop(n_full, n_need, functools.partial(new_body, True), 0)

    lam_rows = lam_ref[...]
    lam = (jnp.exp(jnp.sum(lam_rows[0:1] * lam_rows[1:2], axis=-1, keepdims=True))
           - jnp.exp(jnp.sum(lam_rows[2:3] * lam_rows[3:4], axis=-1, keepdims=True)) + li)
    for h, hs in enumerate(heads):
        o_all = acc_ref[h] / l_ref[h]
        o = o_all[0:tq] - lam * o_all[tq:2 * tq]
        o_ref[0, :, hs] = (_rms(o, sub_ref[...]) * (1.0 - li)).astype(bf16)


def _diff_attention(q, k, v, lam_rows, subln, li, past=None, past_row0=0, hp=2, tq=512, tk=512):
    b, l, d = q.shape
    n_heads = d // V_DIM
    assert n_heads % hp == 0
    w = hp * V_DIM
    tq = _token_tile(l, tq)
    tk = _token_tile(l, tk)
    past_len = 0 if past is None else past[0].shape[2]
    qtile = pl.BlockSpec((1, tq, w), lambda i, g, j: (i, j, g))
    seq = pl.BlockSpec((1, l, w), lambda i, g, j: (i, 0, g))
    args = [lam_rows, subln.reshape(1, V_DIM), q, k, v]
    specs = [_resident(lam_rows.shape), _resident((1, V_DIM)), qtile, seq, seq]
    if past_len:
        assert hp == n_heads and past[1].shape[1:] == (past_len * n_heads, V_DIM)
        args += list(past)
        specs += [pl.BlockSpec((1, d, past_len), lambda i, g, j: (past_row0 + i, 0, 0)),
                  pl.BlockSpec((1, past_len * n_heads, V_DIM), lambda i, g, j: (past_row0 + i, 0, 0))]
    kern = functools.partial(_attn_kernel, hp=hp, tq=tq, tk=tk, past_len=past_len,
                             n_new=l // tk, li=li)
    stat = pltpu.VMEM((hp, 2 * tq, V_DIM), f32)
    return pl.pallas_call(
        kern, out_shape=jax.ShapeDtypeStruct((b, l, d), bf16), grid=(b, n_heads // hp, l // tq),
        in_specs=specs, out_specs=qtile,
        scratch_shapes=[pltpu.VMEM((hp, 2 * tq, V_DIM), bf16), stat, stat, stat],
        compiler_params=_params(3), name="diff_attn")(*args)


def _lambda_init(layer):
    return 0.8 - 0.6 * math.exp(-0.3 * layer)


def _trunk(x, cache_k, cache_v, state_conv, w):
    b, l, d = x.shape
    depth = w['ffn1_norm'].shape[0]
    n_attn = len([i for i in range(depth) if i % N_MIXERS == 1])
    k_major = l % V_DIM == 0
    x2 = x.reshape(b * l, d)
    kv_stacks, new_c = None, []
    for i in range(depth):
        j = i // N_MIXERS
        x2 = _ffn(x2, w['ffn1_norm'][i], *w['ffn1'], i)
        mix = None
        if i % N_MIXERS == 0:
            past = (jnp.zeros((b, w['conv_w_dw'].shape[1] - 1, d), f32) if state_conv is None
                    else state_conv[j])
            conv_w = (w['mix_norm'][i], w['conv_w_pw1'][j], w['conv_b_pw1'][j], w['conv_w_dw'][j],
                      w['conv_b_dw'][j], w['conv_ln_g'][j], w['conv_ln_b'][j], w['conv_w_pw2'][j],
                      w['conv_b_pw2'][j])
            y, c_new = _conv_mixer(x2.reshape(b, l, d), past, *conv_w)
            x2 = y.reshape(b * l, d)
            new_c.append(c_new)
        else:
            q, k_all, v_all, kb, vb = _qkv(x2, w['mix_norm'][i], w['attn_w_qkv'][j], l, k_major,
                                           j, n_attn, kv_stacks)
            kv_stacks = (k_all, v_all)
            past, cfg = None, dict(hp=4)
            if cache_k is not None:
                p_len = cache_k.shape[2]
                past = (jnp.transpose(cache_k, (0, 1, 3, 4, 2)).reshape(n_attn * b, d, p_len),
                        cache_v.reshape(n_attn * b, p_len * N_HEADS, V_DIM))
                cfg = dict(hp=N_HEADS, past_row0=j * b)
            o = _diff_attention(q.reshape(b, l, d), kb.reshape(b, l, d), vb.reshape(b, l, d),
                                w['attn_lambda'][j], w['attn_subln'][j], _lambda_init(i), past, **cfg)
            mix = (o.reshape(b * l, d), w['attn_w_o'][j])
        x2 = _ffn(x2, w['ffn2_norm'][i], *w['ffn2'], i, mix=mix,
                  final_g=w['final_norm'] if i == depth - 1 else None)
    k_all, v_all = kv_stacks
    if k_major:
        k_all = jnp.transpose(k_all.reshape(n_attn, b, 2 * N_HEADS, HEAD_DIM, l), (0, 1, 4, 2, 3))
    return (x2.reshape(b, l, d), k_all.reshape(n_attn, b, l, 2 * N_HEADS, HEAD_DIM),
            v_all.reshape(n_attn, b, l, N_HEADS, V_DIM), jnp.stack(new_c))


def kernel(x_prompt, x_sample, cache_k, cache_v, state_conv, ffn1_norm, ffn1_w_gate, ffn1_w_up, ffn1_w_down, mix_norm, ffn2_norm, ffn2_w_gate, ffn2_w_up, ffn2_w_down, conv_w_pw1, conv_b_pw1, conv_w_dw, conv_b_dw, conv_ln_g, conv_ln_b, conv_w_pw2, conv_b_pw2, attn_w_qkv, attn_lambda_q1, attn_lambda_k1, attn_lambda_q2, attn_lambda_k2, attn_subln, attn_w_o, final_norm):
    w = {
        'ffn1_norm': ffn1_norm, 'mix_norm': mix_norm, 'ffn2_norm': ffn2_norm, 'final_norm': final_norm,
        'ffn1': (ffn1_w_gate.astype(bf16), ffn1_w_up.astype(bf16), ffn1_w_down.astype(bf16)),
        'ffn2': (ffn2_w_gate.astype(bf16), ffn2_w_up.astype(bf16), ffn2_w_down.astype(bf16)),
        'conv_w_pw1': conv_w_pw1.astype(bf16), 'conv_b_pw1': conv_b_pw1, 'conv_w_dw': conv_w_dw,
        'conv_b_dw': conv_b_dw, 'conv_ln_g': conv_ln_g, 'conv_ln_b': conv_ln_b,
        'conv_w_pw2': conv_w_pw2.astype(bf16), 'conv_b_pw2': conv_b_pw2,
        'attn_w_qkv': attn_w_qkv.astype(bf16), 'attn_w_o': attn_w_o.astype(bf16),
        'attn_subln': attn_subln,
        'attn_lambda': jnp.stack([attn_lambda_q1, attn_lambda_k1, attn_lambda_q2, attn_lambda_k2], axis=1),
    }
    y_p, k_p, v_p, c_p = _trunk(x_prompt, None, None, None, w)
    y_s, k_s, v_s, c_s = _trunk(x_sample, cache_k, cache_v, state_conv, w)
    return (y_p, y_s, k_p, v_p, c_p, k_s, v_s, c_s)
```

```python
import functools
import math

import jax
import jax.numpy as jnp
from jax import lax
from jax.experimental import pallas as pl
from jax.experimental.pallas import tpu as pltpu

EPS = 1e-6
FFN_RES = 0.5
CHUNK = 64
N_HEADS = 8
HEAD_DIM = 64
V_DIM = 2 * HEAD_DIM
N_MIXERS = 2
NEG_BIG = -1e30

V7X_VMEM_BYTES = 64 * 1024 * 1024
VMEM_LIMIT_BYTES = V7X_VMEM_BYTES - 8 * 1024 * 1024
FFN_F_TILE = 256
CONV_HALO = 32
CONV_ROW_BLOCK = 64
CONV_LANE_BLOCK = 128
SUBLANES = 8

bf16 = jnp.bfloat16
f32 = jnp.float32


def _params(n_grid_axes):
    return pltpu.CompilerParams(
        dimension_semantics=("arbitrary",) * n_grid_axes,
        vmem_limit_bytes=VMEM_LIMIT_BYTES)


def _resident(shape):
    nd = len(shape)
    return pl.BlockSpec(shape, lambda *_: (0,) * nd, pipeline_mode=pl.Buffered(1))


def _rms(x, g):
    return x * lax.rsqrt(jnp.mean(x * x, axis=-1, keepdims=True) + EPS) * g


def _silu(x):
    return x * jax.nn.sigmoid(x)


def _token_tile(n_tokens, want):
    t = min(want, n_tokens)
    assert n_tokens % t == 0, (n_tokens, t)
    return t


def _ffn_chunks(h_ref, wg_ref, wu_ref, wd_ref, acc_ref, n_f):
    for f in range(n_f):
        cs = slice(f * FFN_F_TILE, (f + 1) * FFN_F_TILE)
        h = h_ref[...]
        g = jnp.dot(h, wg_ref[0, :, cs], preferred_element_type=f32)
        u = jnp.dot(h, wu_ref[0, :, cs], preferred_element_type=f32)
        a = (_silu(g) * u).astype(bf16)
        part = jnp.dot(a, wd_ref[0, cs, :], preferred_element_type=f32)
        if f == 0:
            acc_ref[...] = part
        else:
            acc_ref[...] += part


def _ffn_kernel(*refs, n_f, has_mix, has_final):
    refs = list(refs)
    x_ref = refs.pop(0)
    if has_mix:
        o_ref_in, wo_ref = refs.pop(0), refs.pop(0)
    g_ref, wg_ref, wu_ref, wd_ref = refs[:4]
    refs = refs[4:]
    if has_final:
        gf_ref = refs.pop(0)
    y_ref, h_ref, acc_ref = refs

    x = x_ref[...]
    if has_mix:
        x = x + jnp.dot(o_ref_in[...], wo_ref[...], preferred_element_type=f32)
    h_ref[...] = _rms(x, g_ref[...]).astype(bf16)
    _ffn_chunks(h_ref, wg_ref, wu_ref, wd_ref, acc_ref, n_f)
    y = x + FFN_RES * acc_ref[...]
    if has_final:
        y = _rms(y, gf_ref[...])
    y_ref[...] = y


def _ffn_stream_kernel(*refs, has_mix, has_final):
    refs = list(refs)
    x_ref = refs.pop(0)
    if has_mix:
        o_ref_in, wo_ref = refs.pop(0), refs.pop(0)
    g_ref, wg_ref, wu_ref, wd_ref = refs[:4]
    refs = refs[4:]
    if has_final:
        gf_ref = refs.pop(0)
    y_ref, xs_ref, h_ref, acc_ref = refs
    f = pl.program_id(0)

    @pl.when(f == 0)
    def _():
        x = x_ref[...]
        if has_mix:
            x = x + jnp.dot(o_ref_in[...], wo_ref[...], preferred_element_type=f32)
        xs_ref[...] = x
        h_ref[...] = _rms(x, g_ref[...]).astype(bf16)
        acc_ref[...] = jnp.zeros_like(acc_ref)

    h = h_ref[...]
    g = jnp.dot(h, wg_ref[0], preferred_element_type=f32)
    u = jnp.dot(h, wu_ref[0], preferred_element_type=f32)
    a = (_silu(g) * u).astype(bf16)
    acc_ref[...] += jnp.dot(a, wd_ref[0], preferred_element_type=f32)

    @pl.when(f == pl.num_programs(0) - 1)
    def _():
        y = xs_ref[...] + FFN_RES * acc_ref[...]
        if has_final:
            y = _rms(y, gf_ref[...])
        y_ref[...] = y


def _layer_resident(stack, layer):
    nd = stack.ndim
    return pl.BlockSpec((1,) + stack.shape[1:], lambda *_: (layer,) + (0,) * (nd - 1),
                        pipeline_mode=pl.Buffered(1))


def _ffn(x2, g, wg, wu, wd, layer, mix=None, final_g=None, tm=512):
    t, d = x2.shape
    f_dim = wg.shape[2]
    n_f = f_dim // FFN_F_TILE
    assert n_f * FFN_F_TILE == f_dim
    tm = _token_tile(t, tm)
    stream = t == tm
    row = pl.BlockSpec((tm, d), lambda i: (0, 0) if stream else (i, 0))
    vec = _resident((1, d))
    args, specs = [x2], [row]
    if mix is not None:
        o, wo = mix
        args += [o, wo]
        specs += [pl.BlockSpec((tm, o.shape[1]), lambda i: (0, 0) if stream else (i, 0)),
                  _resident(wo.shape)]
    args += [g.reshape(1, d), wg, wu, wd]
    if stream:
        col_chunk = pl.BlockSpec((1, d, FFN_F_TILE), lambda f: (layer, 0, f))
        specs += [vec, col_chunk, col_chunk, pl.BlockSpec((1, FFN_F_TILE, d), lambda f: (layer, f, 0))]
    else:
        specs += [vec, _layer_resident(wg, layer), _layer_resident(wu, layer), _layer_resident(wd, layer)]
    if final_g is not None:
        args.append(final_g.reshape(1, d))
        specs.append(vec)
    flags = dict(has_mix=mix is not None, has_final=final_g is not None)
    if stream:
        return pl.pallas_call(
            functools.partial(_ffn_stream_kernel, **flags),
            out_shape=jax.ShapeDtypeStruct((t, d), f32), grid=(n_f,), in_specs=specs, out_specs=row,
            scratch_shapes=[pltpu.VMEM((tm, d), f32), pltpu.VMEM((tm, d), bf16), pltpu.VMEM((tm, d), f32)],
            compiler_params=_params(1), name="ffn_stream")(*args)
    kern = functools.partial(_ffn_kernel, n_f=n_f, **flags)
    return pl.pallas_call(
        kern, out_shape=jax.ShapeDtypeStruct((t, d), f32), grid=(t // tm,),
        in_specs=specs, out_specs=row,
        scratch_shapes=[pltpu.VMEM((tm, d), bf16), pltpu.VMEM((tm, d), f32)],
        compiler_params=_params(1), name="ffn")(*args)


def _conv_halo_init(past_ref, ubuf_ref, d, kw):
    ubuf_ref[0:CONV_HALO, :] = jnp.zeros((CONV_HALO, d), f32)
    ubuf_ref[CONV_HALO - (kw - 1):CONV_HALO, :] = past_ref[0]


def _conv_halo_carry(ns_ref, ubuf_ref, tm, kw):
    ns_ref[0] = ubuf_ref[tm + CONV_HALO - (kw - 1):tm + CONV_HALO, :]
    ubuf_ref[0:CONV_HALO, :] = ubuf_ref[tm:tm + CONV_HALO, :]


def _conv_front(x, w_refs, ubuf_ref, tm, d):
    g_ref, w1_ref, b1_ref = w_refs[:3]
    h = _rms(x, g_ref[...]).astype(bf16)
    a = jnp.dot(h, w1_ref[...], preferred_element_type=f32) + b1_ref[...]
    ubuf_ref[CONV_HALO:CONV_HALO + tm, :] = a[:, :d] * jax.nn.sigmoid(a[:, d:])


def _conv_block_fns(w_refs, ubuf_ref, cbuf_ref, tm, d, kw):
    wdw_ref, bdw_ref = w_refs[3:5]
    off = CONV_HALO - (kw - 1)
    rb = min(CONV_ROW_BLOCK, tm)

    def block(r0, c0):
        cs = slice(c0, c0 + CONV_LANE_BLOCK)
        acc = jnp.broadcast_to(bdw_ref[:, cs], (rb, CONV_LANE_BLOCK))
        for rho in range(SUBLANES):
            taps = [k for k in range(kw) if (off + k) % SUBLANES == rho]
            if not taps:
                continue
            span = rb + (off + taps[-1]) // SUBLANES * SUBLANES
            if rho == 0:
                win = ubuf_ref[r0:r0 + span, cs]
            else:
                n = span + SUBLANES
                win = pltpu.roll(ubuf_ref[r0:r0 + n, cs], n - rho, axis=0)
            for k in taps:
                a0 = (off + k) // SUBLANES * SUBLANES
                acc = acc + wdw_ref[k:k + 1, cs] * win[a0:a0 + rb]
        cbuf_ref[r0:r0 + rb, cs] = acc

    return [functools.partial(block, r0, c0)
            for r0 in range(0, tm, rb) for c0 in range(0, d, CONV_LANE_BLOCK)]


def _conv_back(x, w_refs, cbuf_ref):
    lng_ref, lnb_ref, w2_ref, b2_ref = w_refs[5:]
    c = cbuf_ref[...]
    xc = c - jnp.mean(c, axis=-1, keepdims=True)
    ln = xc * lax.rsqrt(jnp.mean(xc * xc, axis=-1, keepdims=True) + EPS) * lng_ref[...] + lnb_ref[...]
    m = jnp.dot(_silu(ln).astype(bf16), w2_ref[...], preferred_element_type=f32) + b2_ref[...]
    return x + m


def _conv_tile(x, w_refs, ubuf_ref, cbuf_ref, tm, d, kw):
    _conv_front(x, w_refs, ubuf_ref, tm, d)
    for fn in _conv_block_fns(w_refs, ubuf_ref, cbuf_ref, tm, d, kw):
        fn()
    return _conv_back(x, w_refs, cbuf_ref)


def _conv_kernel(x_ref, past_ref, *refs, tm, d, kw):
    w_refs = refs[:9]
    y_ref, ns_ref, ubuf_ref, cbuf_ref = refs[9:]

    @pl.when(pl.program_id(1) == 0)
    def _():
        _conv_halo_init(past_ref, ubuf_ref, d, kw)

    y_ref[0] = _conv_tile(x_ref[0], w_refs, ubuf_ref, cbuf_ref, tm, d, kw)
    _conv_halo_carry(ns_ref, ubuf_ref, tm, kw)


def _conv_mixer(x, past, g, w1, b1, wdw, bdw, lng, lnb, w2, b2, tm=512):
    b, l, d = x.shape
    kw = wdw.shape[0]
    assert kw - 1 <= CONV_HALO
    tm = _token_tile(l, tm)
    assert tm >= CONV_HALO and tm % 8 == 0
    tile = pl.BlockSpec((1, tm, d), lambda i, j: (i, j, 0))
    state = pl.BlockSpec((1, kw - 1, d), lambda i, j: (i, 0, 0))
    vec = _resident((1, d))
    kern = functools.partial(_conv_kernel, tm=tm, d=d, kw=kw)
    return pl.pallas_call(
        kern,
        out_shape=(jax.ShapeDtypeStruct((b, l, d), f32), jax.ShapeDtypeStruct((b, kw - 1, d), f32)),
        grid=(b, l // tm),
        in_specs=[tile, state, vec, _resident(w1.shape), _resident((1, 2 * d)), _resident(wdw.shape),
                  vec, vec, vec, _resident(w2.shape), vec],
        out_specs=(tile, state),
        scratch_shapes=[pltpu.VMEM((CONV_HALO + tm, d), f32), pltpu.VMEM((tm, d), f32)],
        compiler_params=_params(2), name="conv_mixer",
    )(x, past, g.reshape(1, d), w1, b1.reshape(1, 2 * d), wdw, bdw.reshape(1, d),
      lng.reshape(1, d), lnb.reshape(1, d), w2, b2.reshape(1, d))


def _qkv_kernel(*refs, d, k_major, own, zero_fill):
    x_ref, g_ref, w_ref = refs[:3]
    q_ref, k_ref, v_ref, kb_ref, vb_ref = refs[-5:]
    h = _rms(x_ref[...], g_ref[...]).astype(bf16)
    q = jnp.dot(h, w_ref[:, 0:d], preferred_element_type=f32)
    q_ref[...] = (q * (HEAD_DIM ** -0.5)).astype(bf16)
    k = jnp.dot(h, w_ref[:, d:2 * d], preferred_element_type=f32)
    kb_ref[...] = k.astype(bf16)
    if k_major:
        k_ref[own, 0] = k.T
    else:
        k_ref[own] = k
    v = jnp.dot(h, w_ref[:, 2 * d:3 * d], preferred_element_type=f32)
    v_ref[own] = v
    vb_ref[...] = v.astype(bf16)
    for other in zero_fill:
        k_ref[other] = jnp.zeros(k_ref.shape[1:], f32)
        v_ref[other] = jnp.zeros(v_ref.shape[1:], f32)


def _qkv(x2, g, w, seq_len, k_major, layer, n_layers, stacks=None, tm=512):
    t, d = x2.shape
    tm = _token_tile(seq_len if k_major else t, tm)
    nt = seq_len // tm
    nb = t // seq_len
    row = pl.BlockSpec((tm, d), lambda i: (i, 0))
    args, specs = [x2, g.reshape(1, d), w], [row, _resident((1, d)), _resident(w.shape)]
    aliases = {}
    if stacks is None:
        lead, first, own, zero_fill = n_layers, 0, layer, [j for j in range(n_layers) if j != layer]
    else:
        lead, first, own, zero_fill = 1, layer, 0, []
        aliases = {len(args): 1, len(args) + 1: 2}
        args += list(stacks)
        specs += [pl.BlockSpec(memory_space=pl.ANY)] * 2
    if k_major:
        k_shape = (n_layers, nb, d, seq_len)
        k_spec = pl.BlockSpec((lead, 1, d, tm), lambda i: (first, i // nt, 0, i % nt))
    else:
        k_shape = (n_layers, t, d)
        k_spec = pl.BlockSpec((lead, tm, d), lambda i: (first, i, 0))
    v_spec = pl.BlockSpec((lead, tm, d), lambda i: (first, i, 0))
    return pl.pallas_call(
        functools.partial(_qkv_kernel, d=d, k_major=k_major, own=own, zero_fill=zero_fill),
        out_shape=(jax.ShapeDtypeStruct((t, d), bf16), jax.ShapeDtypeStruct(k_shape, f32),
                   jax.ShapeDtypeStruct((n_layers, t, d), f32), jax.ShapeDtypeStruct((t, d), bf16),
                   jax.ShapeDtypeStruct((t, d), bf16)),
        grid=(t // tm,),
        in_specs=specs, out_specs=(row, k_spec, v_spec, row, row),
        input_output_aliases=aliases,
        compiler_params=_params(1), name="qkv")(*args)


def _chunk_of(pos):
    assert CHUNK & (CHUNK - 1) == 0
    return lax.shift_right_logical(pos, CHUNK.bit_length() - 1)


def _attn_kernel(*refs, hp, tq, tk, past_len, n_new, li):
    refs = list(refs)
    lam_ref, sub_ref, q_ref, k_ref, v_ref = refs[:5]
    refs = refs[5:]
    if past_len:
        pk_ref, pv_ref = refs[:2]
        refs = refs[2:]
    o_ref, qs_ref, m_ref, l_ref, acc_ref = refs
    qi = pl.program_id(2)
    q_pos0 = past_len + qi * tq
    heads = [slice(h * V_DIM, (h + 1) * V_DIM) for h in range(hp)]

    for h, hs in enumerate(heads):
        q = q_ref[0, :, hs]
        lane = lax.broadcasted_iota(jnp.int32, q.shape, 1)
        qs_ref[h, 0:tq, :] = jnp.where(lane < HEAD_DIM, q, jnp.zeros_like(q))
        qs_ref[h, tq:2 * tq, :] = jnp.where(lane >= HEAD_DIM, q, jnp.zeros_like(q))
    m_ref[...] = jnp.full_like(m_ref, NEG_BIG)
    l_ref[...] = jnp.zeros_like(l_ref)
    acc_ref[...] = jnp.zeros_like(acc_ref)

    def scores(h, kt, k_pos0, k_on_rows):
        s = lax.dot_general(qs_ref[h], kt, (((1,), (0 if k_on_rows else 1,)), ((), ())),
                            preferred_element_type=f32)
        if k_pos0 is not None:
            row = lax.broadcasted_iota(jnp.int32, s.shape, 0)
            qp = q_pos0 + jnp.where(row >= tq, row - tq, row)
            kp = k_pos0 + lax.broadcasted_iota(jnp.int32, s.shape, 1)
            s = jnp.where(_chunk_of(kp) <= _chunk_of(qp), s, NEG_BIG)
        return s

    def absorb(h, s, vt):
        n = s.shape[1]
        m_prev = m_ref[h]
        m_new = jnp.maximum(m_prev, jnp.max(s, axis=-1, keepdims=True))
        alpha = jnp.exp(m_prev - m_new)
        if n % V_DIM == 0:
            m_wide = m_new if n == V_DIM else jnp.concatenate([m_new] * (n // V_DIM), axis=1)
        else:
            assert n < V_DIM
            m_wide = m_new[:, :n]
        p = jnp.exp(s - m_wide).astype(bf16)
        ones = jnp.ones((n, V_DIM), bf16)
        pv = jnp.dot(p, jnp.concatenate([vt, ones], axis=1), preferred_element_type=f32)
        acc_ref[h] = alpha * acc_ref[h] + pv[:, :V_DIM]
        l_ref[h] = alpha * l_ref[h] + pv[:, V_DIM:]
        m_ref[h] = m_new

    def update(units, k_on_rows=False):
        score_of = lambda u: scores(u[0], u[1](), u[3], k_on_rows)
        nxt = score_of(units[0])
        for i, u in enumerate(units):
            cur, nxt = nxt, (score_of(units[i + 1]) if i + 1 < len(units) else None)
            absorb(u[0], cur, u[2]())

    if past_len:
        update([(h, functools.partial(lambda h: pk_ref[0, heads[h], :].astype(bf16), h),
                 functools.partial(lambda h: pv_ref[0, pl.ds(h, past_len, stride=hp), :].astype(bf16), h),
                 None) for h in range(hp)], k_on_rows=True)

    first_end = (q_pos0 // CHUNK + 1) * CHUNK
    last_end = ((q_pos0 + tq - 1) // CHUNK + 1) * CHUNK
    n_full = jnp.clip((first_end - past_len) // tk, 0, n_new)
    n_need = jnp.clip((last_end - past_len + tk - 1) // tk, 0, n_new)

    def tile_units(j, masked):
        rows = pl.ds(pl.multiple_of(j * tk, tk), tk)
        return [(h, functools.partial(lambda h: k_ref[0, rows, heads[h]], h),
                 functools.partial(lambda h: v_ref[0, rows, heads[h]], h),
                 past_len + j * tk if masked else None) for h in range(hp)]

    def new_body(masked, j, carry):
        update(tile_units(j, masked))
        return carry

    def pair_body(jj, carry):
        update(tile_units(2 * jj, False) + tile_units(2 * jj + 1, False))
        return carry

    n_pairs = n_full // 2
    lax.fori_loop(0, n_pairs, pair_body, 0)
    lax.fori_loop(2 * n_pairs, n_full, functools.partial(new_body, False), 0)
    lax.fori_loop(n_full, n_need, functools.partial(new_body, True), 0)

    lam_rows = lam_ref[...]
    lam = (jnp.exp(jnp.sum(lam_rows[0:1] * lam_rows[1:2], axis=-1, keepdims=True))
           - jnp.exp(jnp.sum(lam_rows[2:3] * lam_rows[3:4], axis=-1, keepdims=True)) + li)
    for h, hs in enumerate(heads):
        o_all = acc_ref[h] / l_ref[h]
        o = o_all[0:tq] - lam * o_all[tq:2 * tq]
        o_ref[0, :, hs] = (_rms(o, sub_ref[...]) * (1.0 - li)).astype(bf16)


def _diff_attention(q, k, v, lam_rows, subln, li, past=None, past_row0=0, hp=2, tq=512, tk=512):
    b, l, d = q.shape
    n_heads = d // V_DIM
    assert n_heads % hp == 0
    w = hp * V_DIM
    tq = _token_tile(l, tq)
    tk = _token_tile(l, tk)
    past_len = 0 if past is None else past[0].shape[2]
    qtile = pl.BlockSpec((1, tq, w), lambda i, g, j: (i, j, g))
    seq = pl.BlockSpec((1, l, w), lambda i, g, j: (i, 0, g))
    args = [lam_rows, subln.reshape(1, V_DIM), q, k, v]
    specs = [_resident(lam_rows.shape), _resident((1, V_DIM)), qtile, seq, seq]
    if past_len:
        assert hp == n_heads and past[1].shape[1:] == (past_len * n_heads, V_DIM)
        args += list(past)
        specs += [pl.BlockSpec((1, d, past_len), lambda i, g, j: (past_row0 + i, 0, 0)),
                  pl.BlockSpec((1, past_len * n_heads, V_DIM), lambda i, g, j: (past_row0 + i, 0, 0))]
    kern = functools.partial(_attn_kernel, hp=hp, tq=tq, tk=tk, past_len=past_len,
                             n_new=l // tk, li=li)
    stat = pltpu.VMEM((hp, 2 * tq, V_DIM), f32)
    return pl.pallas_call(
        kern, out_shape=jax.ShapeDtypeStruct((b, l, d), bf16), grid=(b, n_heads // hp, l // tq),
        in_specs=specs, out_specs=qtile,
        scratch_shapes=[pltpu.VMEM((hp, 2 * tq, V_DIM), bf16), stat, stat, stat],
        compiler_params=_params(3), name="diff_attn")(*args)


def _lambda_init(layer):
    return 0.8 - 0.6 * math.exp(-0.3 * layer)


def _trunk(x, cache_k, cache_v, state_conv, w):
    b, l, d = x.shape
    depth = w['ffn1_norm'].shape[0]
    n_attn = len([i for i in range(depth) if i % N_MIXERS == 1])
    k_major = l % V_DIM == 0
    x2 = x.reshape(b * l, d)
    kv_stacks, new_c = None, []
    for i in range(depth):
        j = i // N_MIXERS
        x2 = _ffn(x2, w['ffn1_norm'][i], *w['ffn1'], i)
        mix = None
        if i % N_MIXERS == 0:
            past = (jnp.zeros((b, w['conv_w_dw'].shape[1] - 1, d), f32) if state_conv is None
                    else state_conv[j])
            conv_w = (w['mix_norm'][i], w['conv_w_pw1'][j], w['conv_b_pw1'][j], w['conv_w_dw'][j],
                      w['conv_b_dw'][j], w['conv_ln_g'][j], w['conv_ln_b'][j], w['conv_w_pw2'][j],
                      w['conv_b_pw2'][j])
            y, c_new = _conv_mixer(x2.reshape(b, l, d), past, *conv_w)
            x2 = y.reshape(b * l, d)
            new_c.append(c_new)
        else:
            q, k_all, v_all, kb, vb = _qkv(x2, w['mix_norm'][i], w['attn_w_qkv'][j], l, k_major,
                                           j, n_attn, kv_stacks)
            kv_stacks = (k_all, v_all)
            past, cfg = None, dict(hp=4)
            if cache_k is not None:
                p_len = cache_k.shape[2]
                past = (jnp.transpose(cache_k, (0, 1, 3, 4, 2)).reshape(n_attn * b, d, p_len),
                        cache_v.reshape(n_attn * b, p_len * N_HEADS, V_DIM))
                cfg = dict(hp=N_HEADS, past_row0=j * b)
            o = _diff_attention(q.reshape(b, l, d), kb.reshape(b, l, d), vb.reshape(b, l, d),
                                w['attn_lambda'][j], w['attn_subln'][j], _lambda_init(i), past, **cfg)
            mix = (o.reshape(b * l, d), w['attn_w_o'][j])
        x2 = _ffn(x2, w['ffn2_norm'][i], *w['ffn2'], i, mix=mix,
                  final_g=w['final_norm'] if i == depth - 1 else None)
    k_all, v_all = kv_stacks
    if k_major:
        k_all = jnp.transpose(k_all.reshape(n_attn, b, 2 * N_HEADS, HEAD_DIM, l), (0, 1, 4, 2, 3))
    return (x2.reshape(b, l, d), k_all.reshape(n_attn, b, l, 2 * N_HEADS, HEAD_DIM),
            v_all.reshape(n_attn, b, l, N_HEADS, V_DIM), jnp.stack(new_c))


def kernel(x_prompt, x_sample, cache_k, cache_v, state_conv, ffn1_norm, ffn1_w_gate, ffn1_w_up, ffn1_w_down, mix_norm, ffn2_norm, ffn2_w_gate, ffn2_w_up, ffn2_w_down, conv_w_pw1, conv_b_pw1, conv_w_dw, conv_b_dw, conv_ln_g, conv_ln_b, conv_w_pw2, conv_b_pw2, attn_w_qkv, attn_lambda_q1, attn_lambda_k1, attn_lambda_q2, attn_lambda_k2, attn_subln, attn_w_o, final_norm):
    w = {
        'ffn1_norm': ffn1_norm, 'mix_norm': mix_norm, 'ffn2_norm': ffn2_norm, 'final_norm': final_norm,
        'ffn1': (ffn1_w_gate.astype(bf16), ffn1_w_up.astype(bf16), ffn1_w_down.astype(bf16)),
        'ffn2': (ffn2_w_gate.astype(bf16), ffn2_w_up.astype(bf16), ffn2_w_down.astype(bf16)),
        'conv_w_pw1': conv_w_pw1.astype(bf16), 'conv_b_pw1': conv_b_pw1, 'conv_w_dw': conv_w_dw,
        'conv_b_dw': conv_b_dw, 'conv_ln_g': conv_ln_g, 'conv_ln_b': conv_ln_b,
        'conv_w_pw2': conv_w_pw2.astype(bf16), 'conv_b_pw2': conv_b_pw2,
        'attn_w_qkv': attn_w_qkv.astype(bf16), 'attn_w_o': attn_w_o.astype(bf16),
        'attn_subln': attn_subln,
        'attn_lambda': jnp.stack([attn_lambda_q1, attn_lambda_k1, attn_lambda_q2, attn_lambda_k2], axis=1),
    }
    y_p, k_p, v_p, c_p = _trunk(x_prompt, None, None, None, w)
    y_s, k_s, v_s, c_s = _trunk(x_sample, cache_k, cache_v, state_conv, w)
    return (y_p, y_s, k_p, v_p, c_p, k_s, v_s, c_s)
```

```python
import functools
import math

import jax
import jax.numpy as jnp
from jax import lax
from jax.experimental import pallas as pl
from jax.experimental.pallas import tpu as pltpu

EPS = 1e-6
FFN_RES = 0.5
CHUNK = 64
N_HEADS = 8
HEAD_DIM = 64
V_DIM = 2 * HEAD_DIM
N_MIXERS = 2
NEG_BIG = -1e30

V7X_VMEM_BYTES = 64 * 1024 * 1024
VMEM_LIMIT_BYTES = V7X_VMEM_BYTES - 8 * 1024 * 1024
FFN_F_TILE = 256
CONV_HALO = 32
CONV_ROW_BLOCK = 64
CONV_LANE_BLOCK = 128
SUBLANES = 8

bf16 = jnp.bfloat16
f32 = jnp.float32


def _params(n_grid_axes):
    return pltpu.CompilerParams(
        dimension_semantics=("arbitrary",) * n_grid_axes,
        vmem_limit_bytes=VMEM_LIMIT_BYTES)


def _resident(shape):
    nd = len(shape)
    return pl.BlockSpec(shape, lambda *_: (0,) * nd, pipeline_mode=pl.Buffered(1))


def _rms(x, g):
    return x * lax.rsqrt(jnp.mean(x * x, axis=-1, keepdims=True) + EPS) * g


def _silu(x):
    return x * jax.nn.sigmoid(x)


def _token_tile(n_tokens, want):
    t = min(want, n_tokens)
    assert n_tokens % t == 0, (n_tokens, t)
    return t


def _ffn_chunks(h_ref, wg_ref, wu_ref, wd_ref, acc_ref, n_f):
    for f in range(n_f):
        cs = slice(f * FFN_F_TILE, (f + 1) * FFN_F_TILE)
        h = h_ref[...]
        g = jnp.dot(h, wg_ref[0, :, cs], preferred_element_type=f32)
        u = jnp.dot(h, wu_ref[0, :, cs], preferred_element_type=f32)
        a = (_silu(g) * u).astype(bf16)
        part = jnp.dot(a, wd_ref[0, cs, :], preferred_element_type=f32)
        if f == 0:
            acc_ref[...] = part
        else:
            acc_ref[...] += part


def _ffn_kernel(*refs, n_f, has_mix, has_final):
    refs = list(refs)
    x_ref = refs.pop(0)
    if has_mix:
        o_ref_in, wo_ref = refs.pop(0), refs.pop(0)
    g_ref, wg_ref, wu_ref, wd_ref = refs[:4]
    refs = refs[4:]
    if has_final:
        gf_ref = refs.pop(0)
    y_ref, h_ref, acc_ref = refs

    x = x_ref[...]
    if has_mix:
        x = x + jnp.dot(o_ref_in[...], wo_ref[...], preferred_element_type=f32)
    h_ref[...] = _rms(x, g_ref[...]).astype(bf16)
    _ffn_chunks(h_ref, wg_ref, wu_ref, wd_ref, acc_ref, n_f)
    y = x + FFN_RES * acc_ref[...]
    if has_final:
        y = _rms(y, gf_ref[...])
    y_ref[...] = y


def _ffn_stream_kernel(*refs, has_mix, has_final):
    refs = list(refs)
    x_ref = refs.pop(0)
    if has_mix:
        o_ref_in, wo_ref = refs.pop(0), refs.pop(0)
    g_ref, wg_ref, wu_ref, wd_ref = refs[:4]
    refs = refs[4:]
    if has_final:
        gf_ref = refs.pop(0)
    y_ref, xs_ref, h_ref, acc_ref = refs
    f = pl.program_id(0)

    @pl.when(f == 0)
    def _():
        x = x_ref[...]
        if has_mix:
            x = x + jnp.dot(o_ref_in[...], wo_ref[...], preferred_element_type=f32)
        xs_ref[...] = x
        h_ref[...] = _rms(x, g_ref[...]).astype(bf16)
        acc_ref[...] = jnp.zeros_like(acc_ref)

    h = h_ref[...]
    g = jnp.dot(h, wg_ref[0], preferred_element_type=f32)
    u = jnp.dot(h, wu_ref[0], preferred_element_type=f32)
    a = (_silu(g) * u).astype(bf16)
    acc_ref[...] += jnp.dot(a, wd_ref[0], preferred_element_type=f32)

    @pl.when(f == pl.num_programs(0) - 1)
    def _():
        y = xs_ref[...] + FFN_RES * acc_ref[...]
        if has_final:
            y = _rms(y, gf_ref[...])
        y_ref[...] = y


def _layer_resident(stack, layer):
    nd = stack.ndim
    return pl.BlockSpec((1,) + stack.shape[1:], lambda *_: (layer,) + (0,) * (nd - 1),
                        pipeline_mode=pl.Buffered(1))


def _ffn(x2, g, wg, wu, wd, layer, mix=None, final_g=None, tm=512):
    t, d = x2.shape
    f_dim = wg.shape[2]
    n_f = f_dim // FFN_F_TILE
    assert n_f * FFN_F_TILE == f_dim
    tm = _token_tile(t, tm)
    stream = t == tm
    row = pl.BlockSpec((tm, d), lambda i: (0, 0) if stream else (i, 0))
    vec = _resident((1, d))
    args, specs = [x2], [row]
    if mix is not None:
        o, wo = mix
        args += [o, wo]
        specs += [pl.BlockSpec((tm, o.shape[1]), lambda i: (0, 0) if stream else (i, 0)),
                  _resident(wo.shape)]
    args += [g.reshape(1, d), wg, wu, wd]
    if stream:
        col_chunk = pl.BlockSpec((1, d, FFN_F_TILE), lambda f: (layer, 0, f))
        specs += [vec, col_chunk, col_chunk, pl.BlockSpec((1, FFN_F_TILE, d), lambda f: (layer, f, 0))]
    else:
        specs += [vec, _layer_resident(wg, layer), _layer_resident(wu, layer), _layer_resident(wd, layer)]
    if final_g is not None:
        args.append(final_g.reshape(1, d))
        specs.append(vec)
    flags = dict(has_mix=mix is not None, has_final=final_g is not None)
    if stream:
        return pl.pallas_call(
            functools.partial(_ffn_stream_kernel, **flags),
            out_shape=jax.ShapeDtypeStruct((t, d), f32), grid=(n_f,), in_specs=specs, out_specs=row,
            scratch_shapes=[pltpu.VMEM((tm, d), f32), pltpu.VMEM((tm, d), bf16), pltpu.VMEM((tm, d), f32)],
            compiler_params=_params(1), name="ffn_stream")(*args)
    kern = functools.partial(_ffn_kernel, n_f=n_f, **flags)
    return pl.pallas_call(
        kern, out_shape=jax.ShapeDtypeStruct((t, d), f32), grid=(t // tm,),
        in_specs=specs, out_specs=row,
        scratch_shapes=[pltpu.VMEM((tm, d), bf16), pltpu.VMEM((tm, d), f32)],
        compiler_params=_params(1), name="ffn")(*args)


def _conv_halo_init(past_ref, ubuf_ref, d, kw):
    ubuf_ref[0:CONV_HALO, :] = jnp.zeros((CONV_HALO, d), f32)
    ubuf_ref[CONV_HALO - (kw - 1):CONV_HALO, :] = past_ref[0]


def _conv_halo_carry(ns_ref, ubuf_ref, tm, kw):
    ns_ref[0] = ubuf_ref[tm + CONV_HALO - (kw - 1):tm + CONV_HALO, :]
    ubuf_ref[0:CONV_HALO, :] = ubuf_ref[tm:tm + CONV_HALO, :]


def _conv_front(x, w_refs, ubuf_ref, tm, d, row0=0):
    g_ref, w1_ref, b1_ref = w_refs[:3]
    h = _rms(x, g_ref[...]).astype(bf16)
    a = jnp.dot(h, w1_ref[...], preferred_element_type=f32) + b1_ref[...]
    ubuf_ref[CONV_HALO + row0:CONV_HALO + row0 + tm, :] = a[:, :d] * jax.nn.sigmoid(a[:, d:])


def _conv_block_fns(w_refs, ubuf_ref, cbuf_ref, tm, d, kw):
    wdw_ref, bdw_ref = w_refs[3:5]
    off = CONV_HALO - (kw - 1)
    rb = min(CONV_ROW_BLOCK, tm)

    def block(r0, c0):
        cs = slice(c0, c0 + CONV_LANE_BLOCK)
        acc = jnp.broadcast_to(bdw_ref[:, cs], (rb, CONV_LANE_BLOCK))
        for rho in range(SUBLANES):
            taps = [k for k in range(kw) if (off + k) % SUBLANES == rho]
            if not taps:
                continue
            span = rb + (off + taps[-1]) // SUBLANES * SUBLANES
            if rho == 0:
                win = ubuf_ref[r0:r0 + span, cs]
            else:
                n = span + SUBLANES
                win = pltpu.roll(ubuf_ref[r0:r0 + n, cs], n - rho, axis=0)
            for k in taps:
                a0 = (off + k) // SUBLANES * SUBLANES
                acc = acc + wdw_ref[k:k + 1, cs] * win[a0:a0 + rb]
        cbuf_ref[r0:r0 + rb, cs] = acc

    return [functools.partial(block, r0, c0)
            for r0 in range(0, tm, rb) for c0 in range(0, d, CONV_LANE_BLOCK)]


def _conv_back(x, w_refs, cbuf_ref, row0=0):
    lng_ref, lnb_ref, w2_ref, b2_ref = w_refs[5:]
    c = cbuf_ref[row0:row0 + x.shape[0], :]
    xc = c - jnp.mean(c, axis=-1, keepdims=True)
    ln = xc * lax.rsqrt(jnp.mean(xc * xc, axis=-1, keepdims=True) + EPS) * lng_ref[...] + lnb_ref[...]
    m = jnp.dot(_silu(ln).astype(bf16), w2_ref[...], preferred_element_type=f32) + b2_ref[...]
    return x + m


def _conv_tile(x, w_refs, ubuf_ref, cbuf_ref, tm, d, kw):
    fns = _conv_block_fns(w_refs, ubuf_ref, cbuf_ref, tm, d, kw)
    if tm % (2 * CONV_ROW_BLOCK):
        _conv_front(x, w_refs, ubuf_ref, tm, d)
        for fn in fns:
            fn()
        return _conv_back(x, w_refs, cbuf_ref)
    half = tm // 2
    x0, x1 = x[:half], x[half:]
    _conv_front(x0, w_refs, ubuf_ref, half, d)
    _conv_front(x1, w_refs, ubuf_ref, half, d, row0=half)
    for fn in fns[:len(fns) // 2]:
        fn()
    y0 = _conv_back(x0, w_refs, cbuf_ref)
    for fn in fns[len(fns) // 2:]:
        fn()
    return jnp.concatenate([y0, _conv_back(x1, w_refs, cbuf_ref, row0=half)], axis=0)


def _conv_kernel(x_ref, past_ref, *refs, tm, d, kw):
    w_refs = refs[:9]
    y_ref, ns_ref, ubuf_ref, cbuf_ref = refs[9:]

    @pl.when(pl.program_id(1) == 0)
    def _():
        _conv_halo_init(past_ref, ubuf_ref, d, kw)

    y_ref[0] = _conv_tile(x_ref[0], w_refs, ubuf_ref, cbuf_ref, tm, d, kw)
    _conv_halo_carry(ns_ref, ubuf_ref, tm, kw)


def _conv_mixer(x, past, g, w1, b1, wdw, bdw, lng, lnb, w2, b2, tm=512):
    b, l, d = x.shape
    kw = wdw.shape[0]
    assert kw - 1 <= CONV_HALO
    tm = _token_tile(l, tm)
    assert tm >= CONV_HALO and tm % 8 == 0
    tile = pl.BlockSpec((1, tm, d), lambda i, j: (i, j, 0))
    state = pl.BlockSpec((1, kw - 1, d), lambda i, j: (i, 0, 0))
    vec = _resident((1, d))
    kern = functools.partial(_conv_kernel, tm=tm, d=d, kw=kw)
    return pl.pallas_call(
        kern,
        out_shape=(jax.ShapeDtypeStruct((b, l, d), f32), jax.ShapeDtypeStruct((b, kw - 1, d), f32)),
        grid=(b, l // tm),
        in_specs=[tile, state, vec, _resident(w1.shape), _resident((1, 2 * d)), _resident(wdw.shape),
                  vec, vec, vec, _resident(w2.shape), vec],
        out_specs=(tile, state),
        scratch_shapes=[pltpu.VMEM((CONV_HALO + tm, d), f32), pltpu.VMEM((tm, d), f32)],
        compiler_params=_params(2), name="conv_mixer",
    )(x, past, g.reshape(1, d), w1, b1.reshape(1, 2 * d), wdw, bdw.reshape(1, d),
      lng.reshape(1, d), lnb.reshape(1, d), w2, b2.reshape(1, d))


def _qkv_kernel(*refs, d, k_major, own, zero_fill):
    x_ref, g_ref, w_ref = refs[:3]
    q_ref, k_ref, v_ref, kb_ref, vb_ref = refs[-5:]
    h = _rms(x_ref[...], g_ref[...]).astype(bf16)
    q = jnp.dot(h, w_ref[:, 0:d], preferred_element_type=f32)
    q_ref[...] = (q * (HEAD_DIM ** -0.5)).astype(bf16)
    k = jnp.dot(h, w_ref[:, d:2 * d], preferred_element_type=f32)
    kb_ref[...] = k.astype(bf16)
    if k_major:
        k_ref[own, 0] = k.T
    else:
        k_ref[own] = k
    v = jnp.dot(h, w_ref[:, 2 * d:3 * d], preferred_element_type=f32)
    v_ref[own] = v
    vb_ref[...] = v.astype(bf16)
    for other in zero_fill:
        k_ref[other] = jnp.zeros(k_ref.shape[1:], f32)
        v_ref[other] = jnp.zeros(v_ref.shape[1:], f32)


def _qkv(x2, g, w, seq_len, k_major, layer, n_layers, stacks=None, tm=512):
    t, d = x2.shape
    tm = _token_tile(seq_len if k_major else t, tm)
    nt = seq_len // tm
    nb = t // seq_len
    row = pl.BlockSpec((tm, d), lambda i: (i, 0))
    args, specs = [x2, g.reshape(1, d), w], [row, _resident((1, d)), _resident(w.shape)]
    aliases = {}
    if stacks is None:
        lead, first, own, zero_fill = n_layers, 0, layer, [j for j in range(n_layers) if j != layer]
    else:
        lead, first, own, zero_fill = 1, layer, 0, []
        aliases = {len(args): 1, len(args) + 1: 2}
        args += list(stacks)
        specs += [pl.BlockSpec(memory_space=pl.ANY)] * 2
    if k_major:
        k_shape = (n_layers, nb, d, seq_len)
        k_spec = pl.BlockSpec((lead, 1, d, tm), lambda i: (first, i // nt, 0, i % nt))
    else:
        k_shape = (n_layers, t, d)
        k_spec = pl.BlockSpec((lead, tm, d), lambda i: (first, i, 0))
    v_spec = pl.BlockSpec((lead, tm, d), lambda i: (first, i, 0))
    return pl.pallas_call(
        functools.partial(_qkv_kernel, d=d, k_major=k_major, own=own, zero_fill=zero_fill),
        out_shape=(jax.ShapeDtypeStruct((t, d), bf16), jax.ShapeDtypeStruct(k_shape, f32),
                   jax.ShapeDtypeStruct((n_layers, t, d), f32), jax.ShapeDtypeStruct((t, d), bf16),
                   jax.ShapeDtypeStruct((t, d), bf16)),
        grid=(t // tm,),
        in_specs=specs, out_specs=(row, k_spec, v_spec, row, row),
        input_output_aliases=aliases,
        compiler_params=_params(1), name="qkv")(*args)


def _chunk_of(pos):
    assert CHUNK & (CHUNK - 1) == 0
    return lax.shift_right_logical(pos, CHUNK.bit_length() - 1)


def _attn_kernel(*refs, hp, tq, tk, past_len, n_new, li):
    refs = list(refs)
    lam_ref, sub_ref, q_ref, k_ref, v_ref = refs[:5]
    refs = refs[5:]
    if past_len:
        pk_ref, pv_ref = refs[:2]
        refs = refs[2:]
    o_ref, qs_ref, m_ref, l_ref, acc_ref = refs
    qi = pl.program_id(2)
    q_pos0 = past_len + qi * tq
    heads = [slice(h * V_DIM, (h + 1) * V_DIM) for h in range(hp)]

    for h, hs in enumerate(heads):
        q = q_ref[0, :, hs]
        lane = lax.broadcasted_iota(jnp.int32, q.shape, 1)
        qs_ref[h, 0:tq, :] = jnp.where(lane < HEAD_DIM, q, jnp.zeros_like(q))
        qs_ref[h, tq:2 * tq, :] = jnp.where(lane >= HEAD_DIM, q, jnp.zeros_like(q))
    m_ref[...] = jnp.full_like(m_ref, NEG_BIG)
    l_ref[...] = jnp.zeros_like(l_ref)
    acc_ref[...] = jnp.zeros_like(acc_ref)

    def scores(h, kt, k_pos0, k_on_rows):
        s = lax.dot_general(qs_ref[h], kt, (((1,), (0 if k_on_rows else 1,)), ((), ())),
                            preferred_element_type=f32)
        if k_pos0 is not None:
            row = lax.broadcasted_iota(jnp.int32, s.shape, 0)
            qp = q_pos0 + jnp.where(row >= tq, row - tq, row)
            kp = k_pos0 + lax.broadcasted_iota(jnp.int32, s.shape, 1)
            s = jnp.where(_chunk_of(kp) <= _chunk_of(qp), s, NEG_BIG)
        return s

    def absorb(h, s, vt):
        n = s.shape[1]
        m_prev = m_ref[h]
        m_new = jnp.maximum(m_prev, jnp.max(s, axis=-1, keepdims=True))
        alpha = jnp.exp(m_prev - m_new)
        if n % V_DIM == 0:
            m_wide = m_new if n == V_DIM else jnp.concatenate([m_new] * (n // V_DIM), axis=1)
        else:
            assert n < V_DIM
            m_wide = m_new[:, :n]
        p = jnp.exp(s - m_wide).astype(bf16)
        ones = jnp.ones((n, V_DIM), bf16)
        pv = jnp.dot(p, jnp.concatenate([vt, ones], axis=1), preferred_element_type=f32)
        acc_ref[h] = alpha * acc_ref[h] + pv[:, :V_DIM]
        l_ref[h] = alpha * l_ref[h] + pv[:, V_DIM:]
        m_ref[h] = m_new

    def update(units, k_on_rows=False):
        score_of = lambda u: scores(u[0], u[1](), u[3], k_on_rows)
        nxt = score_of(units[0])
        for i, u in enumerate(units):
            cur, nxt = nxt, (score_of(units[i + 1]) if i + 1 < len(units) else None)
            absorb(u[0], cur, u[2]())

    if past_len:
        update([(h, functools.partial(lambda h: pk_ref[0, heads[h], :].astype(bf16), h),
                 functools.partial(lambda h: pv_ref[0, pl.ds(h, past_len, stride=hp), :].astype(bf16), h),
                 None) for h in range(hp)], k_on_rows=True)

    first_end = (q_pos0 // CHUNK + 1) * CHUNK
    last_end = ((q_pos0 + tq - 1) // CHUNK + 1) * CHUNK
    n_full = jnp.clip((first_end - past_len) // tk, 0, n_new)
    n_need = jnp.clip((last_end - past_len + tk - 1) // tk, 0, n_new)

    def tile_units(j, masked):
        rows = pl.ds(pl.multiple_of(j * tk, tk), tk)
        return [(h, functools.partial(lambda h: k_ref[0, rows, heads[h]], h),
                 functools.partial(lambda h: v_ref[0, rows, heads[h]], h),
                 past_len + j * tk if masked else None) for h in range(hp)]

    def new_body(masked, j, carry):
        update(tile_units(j, masked))
        return carry

    def pair_body(jj, carry):
        update(tile_units(2 * jj, False) + tile_units(2 * jj + 1, False))
        return carry

    n_pairs = n_full // 2
    lax.fori_loop(0, n_pairs, pair_body, 0)
    lax.fori_loop(2 * n_pairs, n_full, functools.partial(new_body, False), 0)
    lax.fori_loop(n_full, n_need, functools.partial(new_body, True), 0)

    lam_rows = lam_ref[...]
    lam = (jnp.exp(jnp.sum(lam_rows[0:1] * lam_rows[1:2], axis=-1, keepdims=True))
           - jnp.exp(jnp.sum(lam_rows[2:3] * lam_rows[3:4], axis=-1, keepdims=True)) + li)
    for h, hs in enumerate(heads):
        o_all = acc_ref[h] / l_ref[h]
        o = o_all[0:tq] - lam * o_all[tq:2 * tq]
        o_ref[0, :, hs] = (_rms(o, sub_ref[...]) * (1.0 - li)).astype(bf16)


def _diff_attention(q, k, v, lam_rows, subln, li, past=None, past_row0=0, hp=2, tq=512, tk=512):
    b, l, d = q.shape
    n_heads = d // V_DIM
    assert n_heads % hp == 0
    w = hp * V_DIM
    tq = _token_tile(l, tq)
    tk = _token_tile(l, tk)
    past_len = 0 if past is None else past[0].shape[2]
    qtile = pl.BlockSpec((1, tq, w), lambda i, g, j: (i, j, g))
    seq = pl.BlockSpec((1, l, w), lambda i, g, j: (i, 0, g))
    args = [lam_rows, subln.reshape(1, V_DIM), q, k, v]
    specs = [_resident(lam_rows.shape), _resident((1, V_DIM)), qtile, seq, seq]
    if past_len:
        assert hp == n_heads and past[1].shape[1:] == (past_len * n_heads, V_DIM)
        args += list(past)
        specs += [pl.BlockSpec((1, d, past_len), lambda i, g, j: (past_row0 + i, 0, 0)),
                  pl.BlockSpec((1, past_len * n_heads, V_DIM), lambda i, g, j: (past_row0 + i, 0, 0))]
    kern = functools.partial(_attn_kernel, hp=hp, tq=tq, tk=tk, past_len=past_len,
                             n_new=l // tk, li=li)
    stat = pltpu.VMEM((hp, 2 * tq, V_DIM), f32)
    return pl.pallas_call(
        kern, out_shape=jax.ShapeDtypeStruct((b, l, d), bf16), grid=(b, n_heads // hp, l // tq),
        in_specs=specs, out_specs=qtile,
        scratch_shapes=[pltpu.VMEM((hp, 2 * tq, V_DIM), bf16), stat, stat, stat],
        compiler_params=_params(3), name="diff_attn")(*args)


def _lambda_init(layer):
    return 0.8 - 0.6 * math.exp(-0.3 * layer)


def _trunk(x, cache_k, cache_v, state_conv, w):
    b, l, d = x.shape
    depth = w['ffn1_norm'].shape[0]
    n_attn = len([i for i in range(depth) if i % N_MIXERS == 1])
    k_major = l % V_DIM == 0
    x2 = x.reshape(b * l, d)
    kv_stacks, new_c = None, []
    for i in range(depth):
        j = i // N_MIXERS
        x2 = _ffn(x2, w['ffn1_norm'][i], *w['ffn1'], i)
        mix = None
        if i % N_MIXERS == 0:
            past = (jnp.zeros((b, w['conv_w_dw'].shape[1] - 1, d), f32) if state_conv is None
                    else state_conv[j])
            conv_w = (w['mix_norm'][i], w['conv_w_pw1'][j], w['conv_b_pw1'][j], w['conv_w_dw'][j],
                      w['conv_b_dw'][j], w['conv_ln_g'][j], w['conv_ln_b'][j], w['conv_w_pw2'][j],
                      w['conv_b_pw2'][j])
            y, c_new = _conv_mixer(x2.reshape(b, l, d), past, *conv_w)
            x2 = y.reshape(b * l, d)
            new_c.append(c_new)
        else:
            q, k_all, v_all, kb, vb = _qkv(x2, w['mix_norm'][i], w['attn_w_qkv'][j], l, k_major,
                                           j, n_attn, kv_stacks)
            kv_stacks = (k_all, v_all)
            past, cfg = None, dict(hp=4)
            if cache_k is not None:
                p_len = cache_k.shape[2]
                past = (jnp.transpose(cache_k, (0, 1, 3, 4, 2)).reshape(n_attn * b, d, p_len),
                        cache_v.reshape(n_attn * b, p_len * N_HEADS, V_DIM))
                cfg = dict(hp=N_HEADS, past_row0=j * b)
            o = _diff_attention(q.reshape(b, l, d), kb.reshape(b, l, d), vb.reshape(b, l, d),
                                w['attn_lambda'][j], w['attn_subln'][j], _lambda_init(i), past, **cfg)
            mix = (o.reshape(b * l, d), w['attn_w_o'][j])
        x2 = _ffn(x2, w['ffn2_norm'][i], *w['ffn2'], i, mix=mix,
                  final_g=w['final_norm'] if i == depth - 1 else None)
    k_all, v_all = kv_stacks
    if k_major:
        k_all = jnp.transpose(k_all.reshape(n_attn, b, 2 * N_HEADS, HEAD_DIM, l), (0, 1, 4, 2, 3))
    return (x2.reshape(b, l, d), k_all.reshape(n_attn, b, l, 2 * N_HEADS, HEAD_DIM),
            v_all.reshape(n_attn, b, l, N_HEADS, V_DIM), jnp.stack(new_c))


def kernel(x_prompt, x_sample, cache_k, cache_v, state_conv, ffn1_norm, ffn1_w_gate, ffn1_w_up, ffn1_w_down, mix_norm, ffn2_norm, ffn2_w_gate, ffn2_w_up, ffn2_w_down, conv_w_pw1, conv_b_pw1, conv_w_dw, conv_b_dw, conv_ln_g, conv_ln_b, conv_w_pw2, conv_b_pw2, attn_w_qkv, attn_lambda_q1, attn_lambda_k1, attn_lambda_q2, attn_lambda_k2, attn_subln, attn_w_o, final_norm):
    w = {
        'ffn1_norm': ffn1_norm, 'mix_norm': mix_norm, 'ffn2_norm': ffn2_norm, 'final_norm': final_norm,
        'ffn1': (ffn1_w_gate.astype(bf16), ffn1_w_up.astype(bf16), ffn1_w_down.astype(bf16)),
        'ffn2': (ffn2_w_gate.astype(bf16), ffn2_w_up.astype(bf16), ffn2_w_down.astype(bf16)),
        'conv_w_pw1': conv_w_pw1.astype(bf16), 'conv_b_pw1': conv_b_pw1, 'conv_w_dw': conv_w_dw,
        'conv_b_dw': conv_b_dw, 'conv_ln_g': conv_ln_g, 'conv_ln_b': conv_ln_b,
        'conv_w_pw2': conv_w_pw2.astype(bf16), 'conv_b_pw2': conv_b_pw2,
        'attn_w_qkv': attn_w_qkv.astype(bf16), 'attn_w_o': attn_w_o.astype(bf16),
        'attn_subln': attn_subln,
        'attn_lambda': jnp.stack([attn_lambda_q1, attn_lambda_k1, attn_lambda_q2, attn_lambda_k2], axis=1),
    }
    y_p, k_p, v_p, c_p = _trunk(x_prompt, None, None, None, w)
    y_s, k_s, v_s, c_s = _trunk(x_sample, cache_k, cache_v, state_conv, w)
    return (y_p, y_s, k_p, v_p, c_p, k_s, v_s, c_s)
```
